```python
import math, functools
import jax, jax.numpy as jnp
from jax import lax
import numpy as np

D_MODEL = 2048
BATCH = 1
SEQ = 8192
DEPTH = 2
DEC_BATCH = 128
DEC_SEQ = 8
PAST_LEN = 8192
PAGE_SIZE = 128

HEAD_DIM = 128
N_HEADS = D_MODEL // HEAD_DIM
N_HEADS_A = N_HEADS // 2
N_HEADS_B = N_HEADS - N_HEADS_A
N_KV_B = max(1, N_HEADS_B // 4)
GROUP_B = N_HEADS_B // N_KV_B
A_BRANCHES = ((128, 1), (512, 4), (2048, 16))
WIN_A = max(w for w, _ in A_BRANCHES)
WIN_B = 128
BLOCK = 128
N_BUCKETS = 32
MAX_DISTANCE = 2048
CONV_W = 3
D_FF = ((8 * D_MODEL // 3 + 127) // 128) * 128
EPS = 1e-6
NEG = -1e30
SCALE = HEAD_DIM ** -0.5
QA_W = N_HEADS_A * HEAD_DIM
QB_W = N_HEADS_B * HEAD_DIM
KB_W = N_KV_B * HEAD_DIM
MIX_W = QA_W + QB_W
IN_OFFSETS = (QA_W, 2 * QA_W, 3 * QA_W, 3 * QA_W + QB_W, 3 * QA_W + QB_W + KB_W)
IN_W = 3 * QA_W + QB_W + 2 * KB_W

kernel_name = "hybrid_dilated_swa_sink_convffn_step"


def _rmsnorm(x, g):
    xf = x.astype(jnp.float32)
    y = xf * lax.rsqrt(jnp.mean(xf * xf, axis=-1, keepdims=True) + EPS)
    return (y * g.astype(jnp.float32)).astype(x.dtype)


def _t5_bucket(dist):
    dist = jnp.maximum(dist, 0)
    max_exact = N_BUCKETS // 2
    ratio = jnp.log(jnp.maximum(dist, 1).astype(jnp.float32) / max_exact) / math.log(MAX_DISTANCE / max_exact)
    large = max_exact + (ratio * (N_BUCKETS - max_exact)).astype(jnp.int32)
    large = jnp.minimum(large, N_BUCKETS - 1)
    return jnp.where(dist < max_exact, dist, large)


def _modulation(c, w_ada, b_ada):
    mod = jax.nn.silu(c) @ w_ada + b_ada
    return jnp.split(mod[:, None, :], 6, axis=-1)


def _project(h, w_in, gq_a, gk_a, gq_b, gk_b):
    b, t, _ = h.shape
    qa, ka, va, qb, kb, vb = jnp.split(h @ w_in, IN_OFFSETS, axis=-1)
    qa = _rmsnorm(qa.reshape(b, t, N_HEADS_A, HEAD_DIM), gq_a)
    ka = _rmsnorm(ka.reshape(b, t, N_HEADS_A, HEAD_DIM), gk_a)
    va = va.reshape(b, t, N_HEADS_A, HEAD_DIM)
    qb = _rmsnorm(qb.reshape(b, t, N_HEADS_B, HEAD_DIM), gq_b)
    kb = _rmsnorm(kb.reshape(b, t, N_KV_B, HEAD_DIM), gk_b)
    vb = vb.reshape(b, t, N_KV_B, HEAD_DIM)
    return qa, ka, va, qb, kb, vb


def _dilated_branch_prompt(q, k, v, bias_tab, window, dil):
    b, t, h, dh = q.shape
    n = window // dil
    l = t // dil
    nb = -(-l // n)
    lp = nb * n

    def to_blocks(a):
        a = a.reshape(b, l, dil, h, dh).transpose(0, 2, 1, 3, 4)
        a = jnp.pad(a, ((0, 0), (0, 0), (0, lp - l), (0, 0), (0, 0)))
        return a.reshape(b, dil, nb, n, h, dh)

    def with_prev(a):
        prev = jnp.pad(a[:, :, :-1], ((0, 0), (0, 0), (1, 0), (0, 0), (0, 0), (0, 0)))
        return jnp.concatenate([prev, a], axis=3)

    qb = to_blocks(q)
    kc = with_prev(to_blocks(k))
    vc = with_prev(to_blocks(v))
    logits = jnp.einsum('brnqhd,brnkhd->brnhqk', qb.astype(jnp.float32), kc.astype(jnp.float32)) * SCALE
    qi = jnp.arange(n)[:, None]
    ki = jnp.arange(2 * n)[None, :]
    dist = qi + n - ki
    blk = jnp.arange(nb)[:, None, None]
    valid = (dist >= 0) & (dist <= n) & ((blk > 0) | (ki >= n))
    bias = bias_tab.astype(jnp.float32)[_t5_bucket(dist * dil)]
    logits = jnp.where(valid[:, None], logits + bias.transpose(2, 0, 1), NEG)
    m = logits.max(axis=-1)
    e = jnp.exp(logits - m[..., None])
    s = e.sum(axis=-1)
    num = jnp.einsum('brnhqk,brnkhd->brnqhd', e, vc.astype(jnp.float32))

    def back(a):
        a = a.reshape((b, dil, lp) + a.shape[4:])[:, :, :l]
        a = jnp.moveaxis(a, 1, 2)
        return a.reshape((b, t) + a.shape[3:])

    return back(jnp.swapaxes(m, 3, 4)), back(jnp.swapaxes(s, 3, 4)), back(num)


def _dilated_branch_sample(q, kcat, vcat, bias_tab, window, dil, buf_len):
    s_new = q.shape[1]
    n = window // dil
    i = jnp.arange(s_new)[:, None]
    j = jnp.arange(n + 1)
    idx = buf_len + i - j[None, :] * dil
    valid = (idx + PAST_LEN - buf_len) >= 0
    idx = jnp.maximum(idx, 0)
    kg = kcat[:, idx]
    vg = vcat[:, idx]
    logits = jnp.einsum('bshd,bsjhd->bhsj', q.astype(jnp.float32), kg.astype(jnp.float32)) * SCALE
    bias = bias_tab.astype(jnp.float32)[_t5_bucket(j * dil)]
    logits = jnp.where(valid, logits + bias.T[:, None, :], NEG)
    m = logits.max(axis=-1)
    e = jnp.exp(logits - m[..., None])
    s = e.sum(axis=-1)
    num = jnp.einsum('bhsj,bsjhd->bshd', e, vg.astype(jnp.float32))
    return m.transpose(0, 2, 1), s.transpose(0, 2, 1), num


def _combine_branches(parts):
    ms = jnp.stack([p[0] for p in parts])
    ss = jnp.stack([p[1] for p in parts])
    nums = jnp.stack([p[2] for p in parts])
    w = jnp.exp(ms - ms.max(axis=0))
    return (w[..., None] * nums).sum(axis=0) / (w * ss).sum(axis=0)[..., None]


def _sink_softmax_weights(logits, sinks):
    sink = sinks.astype(jnp.float32).reshape(N_KV_B, GROUP_B, 1)
    m = jnp.maximum(logits.max(axis=-1), sink)
    e = jnp.exp(logits - m[..., None])
    denom = e.sum(axis=-1) + jnp.exp(sink - m)
    return e / denom[..., None]


def _swa_sink_prompt(q, k, v, bias_tab, sinks):
    b, t, hq, dh = q.shape
    n = BLOCK
    nb = t // n
    qb = q.reshape(b, nb, n, N_KV_B, GROUP_B, dh)

    def with_prev(a):
        a = a.reshape(b, nb, n, N_KV_B, dh)
        prev = jnp.pad(a[:, :-1], ((0, 0), (1, 0), (0, 0), (0, 0), (0, 0)))
        return jnp.concatenate([prev, a], axis=2)

    kc, vc = with_prev(k), with_prev(v)
    logits = jnp.einsum('bnqhgd,bnchd->bnhgqc', qb.astype(jnp.float32), kc.astype(jnp.float32)) * SCALE
    qi = jnp.arange(n)[:, None]
    ci = jnp.arange(2 * n)[None, :]
    dist = qi + n - ci
    blk = jnp.arange(nb)[:, None, None]
    valid = (dist >= 0) & (dist < WIN_B) & ((blk > 0) | (ci >= n))
    bias = bias_tab.astype(jnp.float32)[_t5_bucket(dist)].transpose(2, 0, 1).reshape(N_KV_B, GROUP_B, n, 2 * n)
    logits = jnp.where(valid[:, None, None], logits + bias, NEG)
    p = _sink_softmax_weights(logits, sinks)
    out = jnp.einsum('bnhgqc,bnchd->bnqhgd', p, vc.astype(jnp.float32))
    return out.reshape(b, t, hq * dh)


def _swa_sink_sample(q, kcat, vcat, bias_tab, sinks, buf_len):
    db, s_new, hq, dh = q.shape
    c = kcat.shape[1]
    qg = q.reshape(db, s_new, N_KV_B, GROUP_B, dh)
    dist = buf_len + jnp.arange(s_new)[:, None] - jnp.arange(c)[None, :]
    valid = (dist >= 0) & (dist < WIN_B)
    logits = jnp.einsum('bshgd,bchd->bhgsc', qg.astype(jnp.float32), kcat.astype(jnp.float32)) * SCALE
    bias = bias_tab.astype(jnp.float32)[_t5_bucket(dist)].transpose(2, 0, 1).reshape(N_KV_B, GROUP_B, s_new, c)
    logits = jnp.where(valid, logits + bias, NEG)
    p = _sink_softmax_weights(logits, sinks)
    out = jnp.einsum('bhgsc,bchd->bshgd', p, vcat.astype(jnp.float32))
    return out.reshape(db, s_new, hq * dh)


def _mixers_prompt(qa, ka, va, qb, kb, vb, bias_a, bias_b, sinks):
    oa = _combine_branches([_dilated_branch_prompt(qa, ka, va, bias_a, w, d) for w, d in A_BRANCHES])
    ob = _swa_sink_prompt(qb, kb, vb, bias_b, sinks)
    return oa, ob


def _mixers_sample(qa, ka, va, qb, kb, vb, bias_a, bias_b, sinks, cache_ak, cache_av, cache_bk, cache_bv):
    la = cache_ak.shape[1]
    lb = cache_bk.shape[1]
    kca = jnp.concatenate([cache_ak.astype(ka.dtype), ka], axis=1)
    vca = jnp.concatenate([cache_av.astype(va.dtype), va], axis=1)
    kcb = jnp.concatenate([cache_bk.astype(kb.dtype), kb], axis=1)
    vcb = jnp.concatenate([cache_bv.astype(vb.dtype), vb], axis=1)
    oa = _combine_branches([_dilated_branch_sample(qa, kca, vca, bias_a, w, d, la) for w, d in A_BRANCHES])
    ob = _swa_sink_sample(qb, kcb, vcb, bias_b, sinks, lb)
    return oa, ob


def _ffn(h, hist, w_gate, w_up, w_conv, b_conv, w_down):
    t = h.shape[1]
    g = h @ w_gate
    g_full = jnp.concatenate([hist.astype(g.dtype), g], axis=1)
    gc = b_conv
    for j in range(CONV_W):
        gc = gc + w_conv[j] * g_full[:, j:j + t]
    y = (jax.nn.gelu(gc, approximate=True) * (h @ w_up)) @ w_down
    return y, g_full[:, -(CONV_W - 1):]


def _layer(x, c, hist, mixer, w_ada, b_ada, g_attn, g_ffn, w_in, gq_a, gk_a, gq_b, gk_b,
           w_out, w_gate, w_up, w_conv, b_conv, w_down):
    b, t, _ = x.shape
    sh_a, sc_a, gt_a, sh_f, sc_f, gt_f = _modulation(c, w_ada, b_ada)
    h = _rmsnorm(x, g_attn) * (1 + sc_a) + sh_a
    qa, ka, va, qb, kb, vb = _project(h, w_in, gq_a, gk_a, gq_b, gk_b)
    oa, ob = mixer(qa, ka, va, qb, kb, vb)
    mix = jnp.concatenate([oa.reshape(b, t, QA_W), ob.reshape(b, t, QB_W)], axis=-1).astype(x.dtype)
    x = x + gt_a * (mix @ w_out)
    h = _rmsnorm(x, g_ffn) * (1 + sc_f) + sh_f
    f, conv_state = _ffn(h, hist, w_gate, w_up, w_conv, b_conv, w_down)
    x = x + gt_f * f
    return x, (ka, va, kb, vb), conv_state


def setup_inputs(seed: int = 0) -> dict:
    key = jax.random.key(seed)
    ks = jax.random.split(key, 32)
    f32 = jnp.float32

    def nrm(k, shape, s):
        return jax.random.normal(k, shape, f32) * s

    la = min(WIN_A, PAST_LEN)
    lb = min(WIN_B, PAST_LEN)
    return {
        "x_prompt": nrm(ks[0], (BATCH, SEQ, D_MODEL), 1.0),
        "x_sample": nrm(ks[1], (DEC_BATCH, DEC_SEQ, D_MODEL), 1.0),
        "c_prompt": nrm(ks[2], (BATCH, D_MODEL), 1.0),
        "c_sample": nrm(ks[3], (DEC_BATCH, D_MODEL), 1.0),
        "cache_a_k": nrm(ks[4], (DEPTH, DEC_BATCH, la, N_HEADS_A, HEAD_DIM), 1.0),
        "cache_a_v": nrm(ks[5], (DEPTH, DEC_BATCH, la, N_HEADS_A, HEAD_DIM), 1.0),
        "cache_b_k": nrm(ks[6], (DEPTH, DEC_BATCH, lb, N_KV_B, HEAD_DIM), 1.0),
        "cache_b_v": nrm(ks[7], (DEPTH, DEC_BATCH, lb, N_KV_B, HEAD_DIM), 1.0),
        "state_conv": nrm(ks[8], (DEPTH, DEC_BATCH, CONV_W - 1, D_FF), 1.0),
        "rel_bias": nrm(ks[9], (N_BUCKETS, N_HEADS), 0.5),
        "w_ada": nrm(ks[10], (DEPTH, D_MODEL, 6 * D_MODEL), 0.5 * D_MODEL ** -0.5),
        "b_ada": nrm(ks[11], (DEPTH, 6 * D_MODEL), 0.01),
        "g_attn": 1.0 + nrm(ks[12], (DEPTH, D_MODEL), 0.01),
        "g_ffn": 1.0 + nrm(ks[13], (DEPTH, D_MODEL), 0.01),
        "w_in": nrm(ks[14], (DEPTH, D_MODEL, IN_W), D_MODEL ** -0.5),
        "g_qn_a": 1.0 + nrm(ks[15], (DEPTH, HEAD_DIM), 0.01),
        "g_kn_a": 1.0 + nrm(ks[16], (DEPTH, HEAD_DIM), 0.01),
        "g_qn_b": 1.0 + nrm(ks[17], (DEPTH, HEAD_DIM), 0.01),
        "g_kn_b": 1.0 + nrm(ks[18], (DEPTH, HEAD_DIM), 0.01),
        "sinks": nrm(ks[19], (DEPTH, N_HEADS_B), 0.5),
        "w_out": nrm(ks[20], (DEPTH, MIX_W, D_MODEL), MIX_W ** -0.5),
        "w_gate": nrm(ks[21], (DEPTH, D_MODEL, D_FF), D_MODEL ** -0.5),
        "w_up": nrm(ks[22], (DEPTH, D_MODEL, D_FF), D_MODEL ** -0.5),
        "w_conv": nrm(ks[23], (DEPTH, CONV_W, D_FF), CONV_W ** -0.5),
        "b_conv": nrm(ks[24], (DEPTH, D_FF), 0.01),
        "w_down": nrm(ks[25], (DEPTH, D_FF, D_MODEL), D_FF ** -0.5),
    }


def reference(x_prompt, x_sample, c_prompt, c_sample, cache_a_k, cache_a_v, cache_b_k, cache_b_v,
              state_conv, rel_bias, w_ada, b_ada, g_attn, g_ffn, w_in, g_qn_a, g_kn_a, g_qn_b,
              g_kn_b, sinks, w_out, w_gate, w_up, w_conv, b_conv, w_down):
    bias_a = rel_bias[:, :N_HEADS_A]
    bias_b = rel_bias[:, N_HEADS_A:]
    seq = x_prompt.shape[1]
    keep_a = min(WIN_A, seq)
    keep_b = min(WIN_B, seq)
    hist_p = jnp.zeros((x_prompt.shape[0], CONV_W - 1, D_FF), x_prompt.dtype)
    yp, ys = x_prompt, x_sample
    pak, pav, pbk, pbv, pconv = [], [], [], [], []
    sak, sav, sbk, sbv, sconv = [], [], [], [], []
    for l in range(DEPTH):
        wl = (w_ada[l], b_ada[l], g_attn[l], g_ffn[l], w_in[l], g_qn_a[l], g_kn_a[l], g_qn_b[l],
              g_kn_b[l], w_out[l], w_gate[l], w_up[l], w_conv[l], b_conv[l], w_down[l])
        mixer_p = functools.partial(_mixers_prompt, bias_a=bias_a, bias_b=bias_b, sinks=sinks[l])
        yp, (ka, va, kb, vb), conv_p = _layer(yp, c_prompt, hist_p, mixer_p, *wl)
        pak.append(ka[:, seq - keep_a:])
        pav.append(va[:, seq - keep_a:])
        pbk.append(kb[:, seq - keep_b:])
        pbv.append(vb[:, seq - keep_b:])
        pconv.append(conv_p)
        mixer_s = functools.partial(_mixers_sample, bias_a=bias_a, bias_b=bias_b, sinks=sinks[l],
                                    cache_ak=cache_a_k[l], cache_av=cache_a_v[l],
                                    cache_bk=cache_b_k[l], cache_bv=cache_b_v[l])
        ys, (ka, va, kb, vb), conv_s = _layer(ys, c_sample, state_conv[l], mixer_s, *wl)
        sak.append(ka)
        sav.append(va)
        sbk.append(kb)
        sbv.append(vb)
        sconv.append(conv_s)
    return (yp, ys,
            jnp.stack(pak), jnp.stack(pav), jnp.stack(pbk), jnp.stack(pbv), jnp.stack(pconv),
            jnp.stack(sak), jnp.stack(sav), jnp.stack(sbk), jnp.stack(sbv), jnp.stack(sconv))
```

```python
import functools
import math

import numpy as np
import jax
import jax.numpy as jnp
from jax import lax
from jax.experimental import pallas as pl
from jax.experimental.pallas import tpu as pltpu

F32 = jnp.float32
BF16 = jnp.bfloat16

LANES = 128
SUBLANES = 8
VMEM_BYTES_V7X = 64 * 1024 * 1024

HEAD_DIM = 128
N_HEADS_A = 8
N_HEADS_B = 8
N_KV_B = 2
GROUP_B = N_HEADS_B // N_KV_B
A_BRANCHES = ((128, 1), (512, 4), (2048, 16))
WIN_A = 2048
WIN_B = 128
BLOCK = 128
N_BUCKETS = 32
MAX_DISTANCE = 2048
CONV_W = 3
EPS = 1e-6
NEG = -1e30
SCALE = HEAD_DIM ** -0.5

QA_W = N_HEADS_A * HEAD_DIM
QB_W = N_HEADS_B * HEAD_DIM
KB_W = N_KV_B * HEAD_DIM
IN_W = 3 * QA_W + QB_W + 2 * KB_W
COL_QA, COL_KA, COL_VA = 0, QA_W // LANES, 2 * QA_W // LANES
COL_QB = 3 * QA_W // LANES
COL_KB = COL_QB + QB_W // LANES
COL_VB = COL_KB + KB_W // LANES

CHUNK_A = WIN_A
CHUNK_B = 1024
FF_TILE = 512


def _vmem_limit(block_bytes):
    return int(min(VMEM_BYTES_V7X - (4 << 20), block_bytes + (12 << 20)))


def _nbytes(shape, dtype):
    return int(np.prod(shape)) * jnp.dtype(dtype).itemsize


def _t5_bucket_np(dist):
    dist = np.maximum(dist, 0)
    max_exact = N_BUCKETS // 2
    ratio = np.log(np.maximum(dist, 1).astype(np.float64) / max_exact) / math.log(MAX_DISTANCE / max_exact)
    large = max_exact + (ratio * (N_BUCKETS - max_exact)).astype(np.int32)
    large = np.minimum(large, N_BUCKETS - 1)
    return np.where(dist < max_exact, dist, large).astype(np.int32)


def _masked_bias(rel_bias_heads, dist, valid):
    b = jnp.take(rel_bias_heads.astype(F32), jnp.asarray(_t5_bucket_np(dist).reshape(-1)), axis=0)
    b = b.reshape(dist.shape + (rel_bias_heads.shape[1],))
    b = jnp.moveaxis(b, -1, 0)
    return jnp.where(jnp.asarray(valid)[None], b, NEG)


def _prompt_bias_tables(rel_bias):
    qi = np.arange(BLOCK)[:, None]
    ki = np.arange(2 * BLOCK)[None, :]
    dist = qi + BLOCK - ki
    tabs = []
    for _, dil in A_BRANCHES:
        valid = (dist >= 0) & (dist <= BLOCK)
        tabs.append(_masked_bias(rel_bias[:, :N_HEADS_A], dist * dil, valid))
    bias_a = jnp.stack(tabs)
    valid_b = (dist >= 0) & (dist < WIN_B)
    bias_b = _masked_bias(rel_bias[:, N_HEADS_A:], dist, valid_b)
    return bias_a, bias_b.reshape(N_KV_B, GROUP_B, BLOCK, 2 * BLOCK)


SA_X_M = (WIN_A - 512) // 16
SA_X_ROWS = SA_X_M * SUBLANES
SA_Y_ROWS = 512
SA_NEW_ROWS = LANES
SA_ROWS = SA_X_ROWS + SA_Y_ROWS + SA_NEW_ROWS
SB_ROWS = 2 * LANES


def _sample_tables(rel_bias, dec_seq, la, lb):
    assert la == WIN_A and lb == WIN_B and dec_seq == SUBLANES
    key_row = np.full((SA_ROWS,), -1, np.int64)
    m = np.arange(SA_X_M)[:, None]
    res = np.arange(SUBLANES)[None, :]
    key_row[:SA_X_ROWS] = (16 * m + res).reshape(-1)
    key_row[SA_X_ROWS:SA_X_ROWS + SA_Y_ROWS] = la - SA_Y_ROWS + np.arange(SA_Y_ROWS)
    key_row[SA_X_ROWS + SA_Y_ROWS:SA_X_ROWS + SA_Y_ROWS + dec_seq] = la + np.arange(dec_seq)
    i = np.arange(dec_seq)[:, None]
    dist = la + i - key_row[None, :]
    real = (key_row >= 0)[None, :] & (dist >= 0)
    count = np.zeros(dist.shape, np.float32)
    for win, dil in A_BRANCHES:
        count += (real & (dist % dil == 0) & (dist <= win)).astype(np.float32)
    mb_a = _masked_bias(rel_bias[:, :N_HEADS_A], np.where(real, dist, 0), count > 0)
    mb_a = mb_a.reshape(N_HEADS_A * dec_seq, SA_ROWS)
    cnt_a = jnp.asarray(np.tile(count[None], (N_HEADS_A, 1, 1)).reshape(N_HEADS_A * dec_seq, SA_ROWS))

    key_b = np.full((SB_ROWS,), -1, np.int64)
    key_b[:lb + dec_seq] = np.arange(lb + dec_seq)
    dist_b = lb + i - key_b[None, :]
    valid_b = (key_b >= 0)[None, :] & (dist_b >= 0) & (dist_b < WIN_B)
    mb_b = _masked_bias(rel_bias[:, N_HEADS_A:], np.where(valid_b, dist_b, 0), valid_b)
    mb_b = mb_b.reshape(N_HEADS_B * dec_seq, SB_ROWS)
    return mb_a, cnt_a, mb_b


def _split_bf16(a):
    hi = a.astype(BF16)
    lo = (a - hi.astype(F32)).astype(BF16)
    return hi, lo


def _mod_kernel(c_ref, w_ref, b_ref, o_ref):
    c = c_ref[...]
    s_hi, s_lo = _split_bf16(c * jax.nn.sigmoid(c))
    w_hi, w_lo = _split_bf16(w_ref[...])
    acc = jnp.dot(s_hi, w_hi, preferred_element_type=F32)
    acc += jnp.dot(s_hi, w_lo, preferred_element_type=F32)
    acc += jnp.dot(s_lo, w_hi, preferred_element_type=F32)
    o_ref[...] = acc + b_ref[...]


def _modulation(c_all, w_ada, b_ada):
    m, d = c_all.shape
    n = w_ada.shape[1]
    tn = 512
    blocks = 2 * (_nbytes((d, tn), F32) + _nbytes((m, tn), F32)) + _nbytes((m, d), F32) * 2
    return pl.pallas_call(
        _mod_kernel,
        grid=(n // tn,),
        in_specs=[pl.BlockSpec((m, d), lambda j: (0, 0)),
                  pl.BlockSpec((d, tn), lambda j: (0, j)),
                  pl.BlockSpec((1, tn), lambda j: (0, j))],
        out_specs=pl.BlockSpec((m, tn), lambda j: (0, j)),
        out_shape=jax.ShapeDtypeStruct((m, n), F32),
        compiler_params=pltpu.CompilerParams(dimension_semantics=("parallel",),
                                             vmem_limit_bytes=_vmem_limit(blocks)),
        name="modulation",
    )(c_all, w_ada, b_ada.reshape(1, n))


def _norm_modulate(x, g, sc, sh):
    ms = jnp.mean(x * x, axis=-1, keepdims=True)
    return (x * lax.rsqrt(ms + EPS) * g) * (1.0 + sc) + sh


def _dot_nt(a, b):
    return lax.dot_general(a, b, (((1,), (1,)), ((), ())), preferred_element_type=F32)


def _proj_kernel(x_ref, sc_ref, sh_ref, g_ref, w_ref, gain_ref, flag_ref, o_ref, h_scr):
    @pl.when(pl.program_id(1) == 0)
    def _():
        h = _norm_modulate(x_ref[...], g_ref[...], sc_ref[...], sh_ref[...])
        h_scr[...] = h.reshape(h_scr.shape).astype(BF16)

    acc = jnp.dot(h_scr[...], w_ref[...], preferred_element_type=F32)
    for c in range(acc.shape[1] // LANES):
        sl = slice(c * LANES, (c + 1) * LANES)
        blk = acc[:, sl]
        ms = jnp.mean(blk * blk, axis=-1, keepdims=True)
        nrm = blk * lax.rsqrt(ms + EPS) * gain_ref[:, sl]
        o_ref[:, sl] = jnp.where(flag_ref[:, sl] > 0.0, nrm, blk)


def _projection(x3, mod3, g_attn, w_in_bf, gain, flag, bb, rb):
    nb, r, d = x3.shape
    tm = bb * rb
    tn = 512
    n_m = (nb // bb) * (r // rb)
    rblocks = r // rb
    xmap = lambda i, j: (i // rblocks, i % rblocks, 0)
    blocks = (2 * (_nbytes((tm, d), F32) + _nbytes((d, tn), BF16) + _nbytes((tm, tn), F32))
              + _nbytes((tm, d), BF16) + 8 * _nbytes((bb, SUBLANES, d), F32))
    return pl.pallas_call(
        _proj_kernel,
        grid=(n_m, IN_W // tn),
        in_specs=[pl.BlockSpec((bb, rb, d), xmap),
                  pl.BlockSpec((bb, 1, d), lambda i, j: (i // rblocks, 0, 1)),
                  pl.BlockSpec((bb, 1, d), lambda i, j: (i // rblocks, 0, 0)),
                  pl.BlockSpec((1, d), lambda i, j: (0, 0)),
                  pl.BlockSpec((d, tn), lambda i, j: (0, j)),
                  pl.BlockSpec((1, tn), lambda i, j: (0, j)),
                  pl.BlockSpec((1, tn), lambda i, j: (0, j))],
        out_specs=pl.BlockSpec((tm, tn), lambda i, j: (i, j)),
        out_shape=jax.ShapeDtypeStruct((nb * r, IN_W), F32),
        scratch_shapes=[pltpu.VMEM((tm, d), BF16)],
        compiler_params=pltpu.CompilerParams(dimension_semantics=("parallel", "arbitrary"),
                                             vmem_limit_bytes=_vmem_limit(blocks)),
        name="projection",
    )(x3, mod3, mod3, g_attn.reshape(1, d), w_in_bf, gain, flag)


def _rows(ref, start, dil):
    if dil == 1:
        return ref[start:start + BLOCK, :]
    return ref[pl.ds(start, BLOCK, stride=dil), :]


def _attn_a_prompt_kernel(q_ref, kc_ref, kp_ref, vc_ref, vp_ref, bias_ref, o_ref, num_scr, m_scr, s_scr):
    first_chunk = pl.program_id(0) == 0
    chunk = q_ref.shape[0]
    col = lax.broadcasted_iota(jnp.int32, (BLOCK, 2 * BLOCK), 1)
    for bi, (_, dil) in enumerate(A_BRANCHES):
        nsub = chunk // (BLOCK * dil)
        bias = bias_ref[bi]
        for r in range(dil):
            last = r + dil * BLOCK * (nsub - 1)
            k_prev = _rows(kp_ref, last, dil).astype(BF16)
            v_prev = _rows(vp_ref, last, dil).astype(BF16)
            for ub in range(nsub):
                start = r + dil * BLOCK * ub
                q = _rows(q_ref, start, dil).astype(BF16)
                k_cur = _rows(kc_ref, start, dil).astype(BF16)
                v_cur = _rows(vc_ref, start, dil).astype(BF16)
                z = _dot_nt(q, jnp.concatenate([k_prev, k_cur], axis=0)) * SCALE + bias
                if ub == 0:
                    z = jnp.where(jnp.logical_and(first_chunk, col < BLOCK), NEG, z)
                m = jnp.max(z, axis=-1, keepdims=True)
                e = jnp.exp(z - m)
                s = jnp.sum(e, axis=-1, keepdims=True)
                num = jnp.dot(e.astype(BF16), jnp.concatenate([v_prev, v_cur], axis=0),
                              preferred_element_type=F32)
                if dil == 1:
                    dst = (bi, slice(start, start + BLOCK), slice(None))
                else:
                    dst = (bi, pl.ds(start, BLOCK, stride=dil), slice(None))
                num_scr[dst] = num
                m_scr[dst] = jnp.broadcast_to(m, (BLOCK, LANES))
                s_scr[dst] = jnp.broadcast_to(s, (BLOCK, LANES))
                k_prev, v_prev = k_cur, v_cur
    m_all = jnp.maximum(jnp.maximum(m_scr[0], m_scr[1]), m_scr[2])
    top = jnp.zeros_like(m_all)
    bot = jnp.zeros_like(m_all)
    for bi in range(len(A_BRANCHES)):
        w = jnp.exp(m_scr[bi] - m_all)
        top += w * num_scr[bi]
        bot += w * s_scr[bi]
    o_ref[...] = (top / bot).astype(o_ref.dtype)


def _attn_a_prompt(qkv, bias_a):
    t = qkv.shape[0]
    n_chunks = t // CHUNK_A
    blk = (CHUNK_A, HEAD_DIM)
    prev = lambda b: jnp.maximum(b - 1, 0)
    blocks = (2 * (5 * _nbytes(blk, F32) + _nbytes((3, BLOCK, 2 * BLOCK), F32) + _nbytes(blk, BF16))
              + 9 * _nbytes(blk, F32))
    return pl.pallas_call(
        _attn_a_prompt_kernel,
        grid=(n_chunks, N_HEADS_A),
        in_specs=[pl.BlockSpec(blk, lambda b, h: (b, COL_QA + h)),
                  pl.BlockSpec(blk, lambda b, h: (b, COL_KA + h)),
                  pl.BlockSpec(blk, lambda b, h: (prev(b), COL_KA + h)),
                  pl.BlockSpec(blk, lambda b, h: (b, COL_VA + h)),
                  pl.BlockSpec(blk, lambda b, h: (prev(b), COL_VA + h)),
                  pl.BlockSpec((3, None, BLOCK, 2 * BLOCK), lambda b, h: (0, h, 0, 0))],
        out_specs=pl.BlockSpec(blk, lambda b, h: (b, h)),
        out_shape=jax.ShapeDtypeStruct((t, QA_W), BF16),
        scratch_shapes=[pltpu.VMEM((3, CHUNK_A, HEAD_DIM), F32)] * 3,
        compiler_params=pltpu.CompilerParams(dimension_semantics=("parallel", "parallel"),
                                             vmem_limit_bytes=_vmem_limit(blocks)),
        name="mixer_a_prompt",
    )(qkv, qkv, qkv, qkv, qkv, bias_a)


def _attn_b_prompt_kernel(sink_ref, q_ref, kc_ref, kp_ref, vc_ref, vp_ref, bias_ref, o_ref):
    first_chunk = pl.program_id(0) == 0
    g = pl.program_id(1)
    col = lax.broadcasted_iota(jnp.int32, (BLOCK, 2 * BLOCK), 1)
    k_prev = kp_ref[...].astype(BF16)
    v_prev = vp_ref[...].astype(BF16)
    for blk in range(q_ref.shape[0] // BLOCK):
        rows = slice(blk * BLOCK, (blk + 1) * BLOCK)
        k_cur = kc_ref[rows, :].astype(BF16)
        v_cur = vc_ref[rows, :].astype(BF16)
        kcat = jnp.concatenate([k_prev, k_cur], axis=0)
        vcat = jnp.concatenate([v_prev, v_cur], axis=0)
        for j in range(GROUP_B):
            cols = slice(j * HEAD_DIM, (j + 1) * HEAD_DIM)
            sink = sink_ref[g * GROUP_B + j]
            z = _dot_nt(q_ref[rows, cols].astype(BF16), kcat) * SCALE + bias_ref[j]
            if blk == 0:
                z = jnp.where(jnp.logical_and(first_chunk, col < BLOCK), NEG, z)
            m = jnp.maximum(jnp.max(z, axis=-1, keepdims=True), sink)
            e = jnp.exp(z - m)
            denom = jnp.sum(e, axis=-1, keepdims=True) + jnp.exp(sink - m)
            num = jnp.dot(e.astype(BF16), vcat, preferred_element_type=F32)
            o_ref[rows, cols] = (num / denom).astype(o_ref.dtype)
        k_prev, v_prev = k_cur, v_cur


def _attn_b_prompt(qkv, bias_b, sinks):
    t = qkv.shape[0]
    n_chunks = t // CHUNK_B
    per = CHUNK_B // BLOCK
    qblk = (CHUNK_B, GROUP_B * HEAD_DIM)
    kblk = (CHUNK_B, HEAD_DIM)
    pblk = (BLOCK, HEAD_DIM)
    qcol = COL_QB // GROUP_B
    prev = lambda b: jnp.maximum(b * per - 1, 0)
    blocks = 2 * (_nbytes(qblk, F32) + 2 * _nbytes(kblk, F32) + 2 * _nbytes(pblk, F32)
                  + _nbytes((GROUP_B, BLOCK, 2 * BLOCK), F32) + _nbytes(qblk, BF16))
    return pl.pallas_call(
        _attn_b_prompt_kernel,
        grid=(n_chunks, N_KV_B),
        in_specs=[pl.BlockSpec(memory_space=pltpu.SMEM),
                  pl.BlockSpec(qblk, lambda b, g: (b, qcol + g)),
                  pl.BlockSpec(kblk, lambda b, g: (b, COL_KB + g)),
                  pl.BlockSpec(pblk, lambda b, g: (prev(b), COL_KB + g)),
                  pl.BlockSpec(kblk, lambda b, g: (b, COL_VB + g)),
                  pl.BlockSpec(pblk, lambda b, g: (prev(b), COL_VB + g)),
                  pl.BlockSpec((None, GROUP_B, BLOCK, 2 * BLOCK), lambda b, g: (g, 0, 0, 0))],
        out_specs=pl.BlockSpec(qblk, lambda b, g: (b, g)),
        out_shape=jax.ShapeDtypeStruct((t, QB_W), BF16),
        compiler_params=pltpu.CompilerParams(dimension_semantics=("parallel", "parallel"),
                                             vmem_limit_bytes=_vmem_limit(blocks)),
        name="mixer_b_prompt",
    )(sinks.astype(F32), qkv, qkv, qkv, qkv, qkv, bias_b)


def _head_block_diag(q, n_heads, n_groups):
    per_group = n_heads // n_groups
    zero = jnp.zeros((q.shape[0], HEAD_DIM), q.dtype)
    rows = []
    for h in range(n_heads):
        qh = q[:, h * HEAD_DIM:(h + 1) * HEAD_DIM]
        rows.append(jnp.concatenate([qh if g == h // per_group else zero for g in range(n_groups)], axis=1))
    return jnp.concatenate(rows, axis=0)


def _head_diag_blocks(o, n_heads, n_groups):
    per_group = n_heads // n_groups
    s = o.shape[0] // n_heads
    return jnp.concatenate(
        [o[h * s:(h + 1) * s, (h // per_group) * HEAD_DIM:(h // per_group + 1) * HEAD_DIM] for h in range(n_heads)],
        axis=1)


def _attn_sample_kernel(qkv_ref, kx_ref, ky_ref, vx_ref, vy_ref, kb_ref, vb_ref,
                        mba_ref, cnt_ref, mbb_ref, sink_ref, o_ref,
                        ka_scr, va_scr, kb_scr, vb_scr):
    s_new = qkv_ref.shape[0]
    pad_a = jnp.zeros((SA_NEW_ROWS - s_new, QA_W), F32)
    pad_b = jnp.zeros((SB_ROWS - WIN_B - s_new, KB_W), F32)
    new_lo = SA_X_ROWS + SA_Y_ROWS

    def fill(scr, x_ref, y_ref, col0):
        scr[0:SA_X_ROWS, :] = x_ref[...].reshape(SA_X_ROWS, QA_W).astype(BF16)
        scr[SA_X_ROWS:new_lo, :] = y_ref[...].astype(BF16)
        new = qkv_ref[:, col0 * LANES:col0 * LANES + QA_W]
        scr[new_lo:SA_ROWS, :] = jnp.concatenate([new, pad_a], axis=0).astype(BF16)

    fill(ka_scr, kx_ref, ky_ref, COL_KA)
    fill(va_scr, vx_ref, vy_ref, COL_VA)
    kb_scr[0:WIN_B, :] = kb_ref[...].astype(BF16)
    kb_scr[WIN_B:SB_ROWS, :] = jnp.concatenate(
        [qkv_ref[:, COL_KB * LANES:COL_KB * LANES + KB_W], pad_b], axis=0).astype(BF16)
    vb_scr[0:WIN_B, :] = vb_ref[...].astype(BF16)
    vb_scr[WIN_B:SB_ROWS, :] = jnp.concatenate(
        [qkv_ref[:, COL_VB * LANES:COL_VB * LANES + KB_W], pad_b], axis=0).astype(BF16)

    qa = _head_block_diag(qkv_ref[:, COL_QA * LANES:COL_QA * LANES + QA_W], N_HEADS_A, N_HEADS_A)
    z = _dot_nt(qa.astype(BF16), ka_scr[...]) * SCALE + mba_ref[...]
    m = jnp.max(z, axis=-1, keepdims=True)
    p = cnt_ref[...] * jnp.exp(z - m)
    denom = jnp.sum(p, axis=-1, keepdims=True)
    oa = jnp.dot(p.astype(BF16), va_scr[...], preferred_element_type=F32) / denom
    o_ref[:, 0:QA_W] = _head_diag_blocks(oa, N_HEADS_A, N_HEADS_A)

    qb = _head_block_diag(qkv_ref[:, COL_QB * LANES:COL_QB * LANES + QB_W], N_HEADS_B, N_KV_B)
    zb = _dot_nt(qb.astype(BF16), kb_scr[...]) * SCALE + mbb_ref[...]
    sink = sink_ref[:, 0:1]
    mb = jnp.maximum(jnp.max(zb, axis=-1, keepdims=True), sink)
    eb = jnp.exp(zb - mb)
    denom_b = jnp.sum(eb, axis=-1, keepdims=True) + jnp.exp(sink - mb)
    ob = jnp.dot(eb.astype(BF16), vb_scr[...], preferred_element_type=F32) / denom_b
    o_ref[:, QA_W:QA_W + QB_W] = _head_diag_blocks(ob, N_HEADS_B, N_KV_B)


def _attn_sample(qkv, cache_ak, cache_av, cache_bk, cache_bv, mb_a, cnt_a, mb_b, sink_rows, s_new):
    db, la = cache_ak.shape[0], cache_ak.shape[1]
    ax_k = cache_ak.reshape(db, la // 16, 16, QA_W)
    ay_k = cache_ak.reshape(db, la, QA_W)
    ax_v = cache_av.reshape(db, la // 16, 16, QA_W)
    ay_v = cache_av.reshape(db, la, QA_W)
    bk = cache_bk.reshape(db, WIN_B, KB_W)
    bv = cache_bv.reshape(db, WIN_B, KB_W)
    xblk = (None, SA_X_M, SUBLANES, QA_W)
    yblk = (None, SA_Y_ROWS, QA_W)
    bblk = (None, WIN_B, KB_W)
    y_idx = (la - SA_Y_ROWS) // SA_Y_ROWS
    rows_q = N_HEADS_A * s_new
    const = lambda b: (0, 0)
    blocks = (2 * (_nbytes((s_new, IN_W), F32) + 2 * _nbytes((SA_X_ROWS, QA_W), F32)
                   + 2 * _nbytes((SA_Y_ROWS, QA_W), F32) + 2 * _nbytes((WIN_B, KB_W), F32)
                   + 2 * _nbytes((rows_q, SA_ROWS), F32) + _nbytes((rows_q, SB_ROWS), F32)
                   + _nbytes((rows_q, LANES), F32) + _nbytes((s_new, QA_W + QB_W), F32))
              + 2 * _nbytes((SA_ROWS, QA_W), BF16) + 2 * _nbytes((SB_ROWS, KB_W), BF16))
    return pl.pallas_call(
        _attn_sample_kernel,
        grid=(db,),
        in_specs=[pl.BlockSpec((s_new, IN_W), lambda b: (b, 0)),
                  pl.BlockSpec(xblk, lambda b: (b, 0, 0, 0)),
                  pl.BlockSpec(yblk, lambda b: (b, y_idx, 0)),
                  pl.BlockSpec(xblk, lambda b: (b, 0, 0, 0)),
                  pl.BlockSpec(yblk, lambda b: (b, y_idx, 0)),
                  pl.BlockSpec(bblk, lambda b: (b, 0, 0)),
                  pl.BlockSpec(bblk, lambda b: (b, 0, 0)),
                  pl.BlockSpec((rows_q, SA_ROWS), const),
                  pl.BlockSpec((rows_q, SA_ROWS), const),
                  pl.BlockSpec((rows_q, SB_ROWS), const),
                  pl.BlockSpec((rows_q, LANES), const)],
        out_specs=pl.BlockSpec((s_new, QA_W + QB_W), lambda b: (b, 0)),
        out_shape=jax.ShapeDtypeStruct((db * s_new, QA_W + QB_W), F32),
        scratch_shapes=[pltpu.VMEM((SA_ROWS, QA_W), BF16), pltpu.VMEM((SA_ROWS, QA_W), BF16),
                        pltpu.VMEM((SB_ROWS, KB_W), BF16), pltpu.VMEM((SB_ROWS, KB_W), BF16)],
        compiler_params=pltpu.CompilerParams(dimension_semantics=("parallel",),
                                             vmem_limit_bytes=_vmem_limit(blocks)),
        name="mixers_sample",
    )(qkv, ax_k, ay_k, ax_v, ay_v, bk, bv, mb_a, cnt_a, mb_b, sink_rows)


def _outproj_kernel(x_ref, ma_ref, mb_ref, wa_ref, wb_ref, gt_ref, o_ref):
    acc = jnp.dot(ma_ref[...].astype(BF16), wa_ref[...], preferred_element_type=F32)
    acc += jnp.dot(mb_ref[...].astype(BF16), wb_ref[...], preferred_element_type=F32)
    o_ref[...] = x_ref[...] + gt_ref[...] * acc.reshape(o_ref.shape)


def _out_projection(x3, mix_a, mix_b, col_a, col_b, w_out_bf, mod3, bb, rb):
    nb, r, d = x3.shape
    tm = bb * rb
    tn = 1024
    rblocks = r // rb
    n_m = (nb // bb) * rblocks
    kh = QA_W
    blocks = 2 * (2 * _nbytes((tm, tn), F32) + _nbytes((tm, kh), mix_a.dtype) + _nbytes((tm, kh), mix_b.dtype)
                  + 2 * _nbytes((kh, tn), BF16) + _nbytes((bb, SUBLANES, tn), F32))
    return pl.pallas_call(
        _outproj_kernel,
        grid=(n_m, d // tn),
        in_specs=[pl.BlockSpec((bb, rb, tn), lambda i, j: (i // rblocks, i % rblocks, j)),
                  pl.BlockSpec((tm, kh), lambda i, j: (i, col_a)),
                  pl.BlockSpec((tm, kh), lambda i, j: (i, col_b)),
                  pl.BlockSpec((kh, tn), lambda i, j: (0, j)),
                  pl.BlockSpec((kh, tn), lambda i, j: (1, j)),
                  pl.BlockSpec((bb, 1, tn), lambda i, j: (i // rblocks, 0, 2 * (d // tn) + j))],
        out_specs=pl.BlockSpec((bb, rb, tn), lambda i, j: (i // rblocks, i % rblocks, j)),
        out_shape=jax.ShapeDtypeStruct(x3.shape, F32),
        compiler_params=pltpu.CompilerParams(dimension_semantics=("parallel", "parallel"),
                                             vmem_limit_bytes=_vmem_limit(blocks)),
        name="out_projection",
    )(x3, mix_a, mix_b, w_out_bf, w_out_bf, mod3)


def _ffn_kernel(x_ref, sc_ref, sh_ref, gt_ref, g_ref, wg_ref, wu_ref, wd_ref, wc_ref, bc_ref, hist_ref,
                o_ref, cs_ref, h_scr, carry_scr, *, rblocks):
    i = pl.program_id(0)
    f = pl.program_id(1)
    n_f = pl.num_programs(1)
    bb, rb, d = x_ref.shape
    tm, tf = bb * rb, wg_ref.shape[1]

    @pl.when(f == 0)
    def _():
        h = _norm_modulate(x_ref[...], g_ref[...], sc_ref[...], sh_ref[...])
        h_scr[...] = h.reshape(tm, d).astype(BF16)

    if rblocks > 1:
        @pl.when(i % rblocks == 0)
        def _():
            carry_scr[f] = hist_ref[...]
        hist = carry_scr[f]
    else:
        hist = hist_ref[...]

    h = h_scr[...]
    gate = jnp.dot(h, wg_ref[...], preferred_element_type=F32)
    up = jnp.dot(h, wu_ref[...], preferred_element_type=F32)
    t = lax.broadcasted_iota(jnp.int32, (bb, rb, tf), 1)
    g3 = gate.reshape(bb, rb, tf)
    gm1 = jnp.where(t >= 1, pltpu.roll(gate, 1, axis=0).reshape(bb, rb, tf), hist[:, 1:2, :])
    gm2 = jnp.where(t >= 2, pltpu.roll(gate, 2, axis=0).reshape(bb, rb, tf),
                    jnp.where(t == 0, hist[:, 0:1, :], hist[:, 1:2, :]))
    gc = bc_ref[...] + wc_ref[0:1, :] * gm2 + wc_ref[1:2, :] * gm1 + wc_ref[2:3, :] * g3
    act = jax.nn.gelu(gc, approximate=True) * up.reshape(bb, rb, tf)
    y = jnp.dot(act.reshape(tm, tf).astype(BF16), wd_ref[...], preferred_element_type=F32).reshape(bb, rb, d)
    tail = g3[:, rb - (CONV_W - 1):, :]
    if rblocks > 1:
        carry_scr[f] = tail
    cs_ref[...] = tail

    @pl.when(f == 0)
    def _():
        o_ref[...] = y

    @pl.when(f > 0)
    def _():
        o_ref[...] += y

    @pl.when(f == n_f - 1)
    def _():
        o_ref[...] = x_ref[...] + gt_ref[...] * o_ref[...]


def _conv_ffn(x3, mod3, g_ffn, wg_bf, wu_bf, wd_bf, w_conv, b_conv, hist, bb, rb):
    nb, r, d = x3.shape
    f_pad = wg_bf.shape[1]
    tf = FF_TILE
    n_f = f_pad // tf
    tm = bb * rb
    rblocks = r // rb
    assert rblocks == 1 or bb == 1
    n_m = (nb // bb) * rblocks
    carry_slots = n_f if rblocks > 1 else 1
    modmap = lambda k: (lambda i, f: (i // rblocks, 0, k))
    blocks = (2 * (2 * _nbytes((tm, d), F32) + 2 * _nbytes((d, tf), BF16) + _nbytes((tf, d), BF16)
                   + 2 * _nbytes((bb, SUBLANES, tf), F32) + 3 * _nbytes((bb, SUBLANES, d), F32))
              + _nbytes((tm, d), BF16) + carry_slots * _nbytes((bb, SUBLANES, tf), F32))
    y, tails = pl.pallas_call(
        functools.partial(_ffn_kernel, rblocks=rblocks),
        grid=(n_m, n_f),
        in_specs=[pl.BlockSpec((bb, rb, d), lambda i, f: (i // rblocks, i % rblocks, 0)),
                  pl.BlockSpec((bb, 1, d), modmap(4)),
                  pl.BlockSpec((bb, 1, d), modmap(3)),
                  pl.BlockSpec((bb, 1, d), modmap(5)),
                  pl.BlockSpec((1, d), lambda i, f: (0, 0)),
                  pl.BlockSpec((d, tf), lambda i, f: (0, f)),
                  pl.BlockSpec((d, tf), lambda i, f: (0, f)),
                  pl.BlockSpec((tf, d), lambda i, f: (f, 0)),
                  pl.BlockSpec((CONV_W, tf), lambda i, f: (0, f)),
                  pl.BlockSpec((1, tf), lambda i, f: (0, f)),
                  pl.BlockSpec((bb, CONV_W - 1, tf), lambda i, f: (i // rblocks, 0, f))],
        out_specs=[pl.BlockSpec((bb, rb, d), lambda i, f: (i // rblocks, i % rblocks, 0)),
                   pl.BlockSpec((bb, CONV_W - 1, tf), lambda i, f: (i, 0, f))],
        out_shape=[jax.ShapeDtypeStruct(x3.shape, F32),
                   jax.ShapeDtypeStruct((n_m * bb, CONV_W - 1, f_pad), F32)],
        scratch_shapes=[pltpu.VMEM((tm, d), BF16),
                        pltpu.VMEM((carry_slots, bb, CONV_W - 1, tf), F32)],
        compiler_params=pltpu.CompilerParams(dimension_semantics=("arbitrary", "arbitrary"),
                                             vmem_limit_bytes=_vmem_limit(blocks)),
        name="conv_ffn",
    )(x3, mod3, mod3, mod3, g_ffn.reshape(1, d), wg_bf, wu_bf, wd_bf, w_conv, b_conv.reshape(1, f_pad), hist)
    state = tails.reshape(nb // bb, rblocks, bb, CONV_W - 1, f_pad)[:, rblocks - 1]
    return y, state.reshape(nb, CONV_W - 1, f_pad)


def _pad_last(a, n):
    return jnp.pad(a, [(0, 0)] * (a.ndim - 1) + [(0, n - a.shape[-1])])


def kernel(x_prompt, x_sample, c_prompt, c_sample, cache_a_k, cache_a_v, cache_b_k, cache_b_v, state_conv, rel_bias, w_ada, b_ada, g_attn, g_ffn, w_in, g_qn_a, g_kn_a, g_qn_b, g_kn_b, sinks, w_out, w_gate, w_up, w_conv, b_conv, w_down):
    depth = w_in.shape[0]
    batch, seq, d = x_prompt.shape
    db, s_new, _ = x_sample.shape
    d_ff = w_gate.shape[2]
    f_pad = -(-d_ff // FF_TILE) * FF_TILE
    la, lb = cache_a_k.shape[2], cache_b_k.shape[2]
    assert batch == 1 and seq % CHUNK_A == 0 and s_new == SUBLANES

    bias_a, bias_b = _prompt_bias_tables(rel_bias)
    mb_a, cnt_a, mb_b = _sample_tables(rel_bias, s_new, la, lb)

    c_all = jnp.concatenate([c_prompt, jnp.zeros((SUBLANES - batch, d), F32), c_sample], axis=0)

    ones = jnp.ones((HEAD_DIM,), F32)
    flag = jnp.concatenate([jnp.ones((2 * QA_W,), F32), jnp.zeros((QA_W,), F32), jnp.ones((QB_W + KB_W,), F32),
                            jnp.zeros((KB_W,), F32)]).reshape(1, IN_W)

    xp = x_prompt
    xs = x_sample
    hist_p = jnp.zeros((batch, CONV_W - 1, f_pad), F32)
    outs = [[] for _ in range(10)]
    for l in range(depth):
        w_in_bf = w_in[l].astype(BF16)
        w_out_bf = w_out[l].astype(BF16)
        wg_bf = _pad_last(w_gate[l], f_pad).astype(BF16)
        wu_bf = _pad_last(w_up[l], f_pad).astype(BF16)
        wd_bf = jnp.pad(w_down[l], ((0, f_pad - d_ff), (0, 0))).astype(BF16)
        wc = _pad_last(w_conv[l], f_pad)
        bc = _pad_last(b_conv[l], f_pad)
        gain = jnp.concatenate([jnp.tile(g_qn_a[l], N_HEADS_A), jnp.tile(g_kn_a[l], N_HEADS_A),
                                jnp.tile(ones, N_HEADS_A), jnp.tile(g_qn_b[l], N_HEADS_B),
                                jnp.tile(g_kn_b[l], N_KV_B), jnp.tile(ones, N_KV_B)]).reshape(1, IN_W)
        sink_rows = jnp.broadcast_to(jnp.repeat(sinks[l].astype(F32), s_new)[:, None], (N_HEADS_B * s_new, LANES))

        mod = _modulation(c_all, w_ada[l], b_ada[l])
        mod_p = mod[0:batch].reshape(batch, 1, 6 * d)
        mod_s = mod[SUBLANES:].reshape(db, 1, 6 * d)

        qkv_p = _projection(xp, mod_p, g_attn[l], w_in_bf, gain, flag, 1, 1024)
        mix_pa = _attn_a_prompt(qkv_p, bias_a)
        mix_pb = _attn_b_prompt(qkv_p, bias_b, sinks[l])
        xp = _out_projection(xp, mix_pa, mix_pb, 0, 0, w_out_bf, mod_p, 1, 1024)
        xp, conv_p = _conv_ffn(xp, mod_p, g_ffn[l], wg_bf, wu_bf, wd_bf, wc, bc, hist_p, 1, 512)
        keep_a, keep_b = min(WIN_A, seq), min(WIN_B, seq)
        outs[0].append(qkv_p[seq - keep_a:, COL_KA * LANES:COL_KA * LANES + QA_W].reshape(batch, keep_a, N_HEADS_A, HEAD_DIM))
        outs[1].append(qkv_p[seq - keep_a:, COL_VA * LANES:COL_VA * LANES + QA_W].reshape(batch, keep_a, N_HEADS_A, HEAD_DIM))
        outs[2].append(qkv_p[seq - keep_b:, COL_KB * LANES:COL_KB * LANES + KB_W].reshape(batch, keep_b, N_KV_B, HEAD_DIM))
        outs[3].append(qkv_p[seq - keep_b:, COL_VB * LANES:COL_VB * LANES + KB_W].reshape(batch, keep_b, N_KV_B, HEAD_DIM))
        outs[4].append(conv_p[:, :, :d_ff])

        qkv_s = _projection(xs, mod_s, g_attn[l], w_in_bf, gain, flag, db, s_new)
        mix_s = _attn_sample(qkv_s, cache_a_k[l], cache_a_v[l], cache_b_k[l], cache_b_v[l],
                             mb_a, cnt_a, mb_b, sink_rows, s_new)
        xs = _out_projection(xs, mix_s, mix_s, 0, 1, w_out_bf, mod_s, db, s_new)
        xs, conv_s = _conv_ffn(xs, mod_s, g_ffn[l], wg_bf, wu_bf, wd_bf, wc, bc,
                               _pad_last(state_conv[l], f_pad), db // 2, s_new)
        outs[5].append(qkv_s[:, COL_KA * LANES:COL_KA * LANES + QA_W].reshape(db, s_new, N_HEADS_A, HEAD_DIM))
        outs[6].append(qkv_s[:, COL_VA * LANES:COL_VA * LANES + QA_W].reshape(db, s_new, N_HEADS_A, HEAD_DIM))
        outs[7].append(qkv_s[:, COL_KB * LANES:COL_KB * LANES + KB_W].reshape(db, s_new, N_KV_B, HEAD_DIM))
        outs[8].append(qkv_s[:, COL_VB * LANES:COL_VB * LANES + KB_W].reshape(db, s_new, N_KV_B, HEAD_DIM))
        outs[9].append(conv_s[:, :, :d_ff])

    return (xp, xs) + tuple(jnp.stack(o) for o in outs)
```

```python
import functools
import math

import numpy as np
import jax
import jax.numpy as jnp
from jax import lax
from jax.experimental import pallas as pl
from jax.experimental.pallas import tpu as pltpu

F32 = jnp.float32
BF16 = jnp.bfloat16

LANES = 128
SUBLANES = 8
VMEM_BYTES_V7X = 64 * 1024 * 1024

HEAD_DIM = 128
N_HEADS_A = 8
N_HEADS_B = 8
N_KV_B = 2
GROUP_B = N_HEADS_B // N_KV_B
A_BRANCHES = ((128, 1), (512, 4), (2048, 16))
WIN_A = 2048
WIN_B = 128
BLOCK = 128
N_BUCKETS = 32
MAX_DISTANCE = 2048
CONV_W = 3
EPS = 1e-6
NEG = -1e30
SCALE = HEAD_DIM ** -0.5

QA_W = N_HEADS_A * HEAD_DIM
QB_W = N_HEADS_B * HEAD_DIM
KB_W = N_KV_B * HEAD_DIM
IN_W = 3 * QA_W + QB_W + 2 * KB_W
COL_QA, COL_KA, COL_VA = 0, QA_W // LANES, 2 * QA_W // LANES
COL_QB = 3 * QA_W // LANES
COL_KB = COL_QB + QB_W // LANES
COL_VB = COL_KB + KB_W // LANES

CHUNK_A = WIN_A
CHUNK_B = 1024
FF_TILE = 512


def _vmem_limit(block_bytes):
    return int(min(VMEM_BYTES_V7X - (4 << 20), block_bytes + (12 << 20)))


def _nbytes(shape, dtype):
    return int(np.prod(shape)) * jnp.dtype(dtype).itemsize


def _t5_bucket_np(dist):
    dist = np.maximum(dist, 0)
    max_exact = N_BUCKETS // 2
    ratio = np.log(np.maximum(dist, 1).astype(np.float64) / max_exact) / math.log(MAX_DISTANCE / max_exact)
    large = max_exact + (ratio * (N_BUCKETS - max_exact)).astype(np.int32)
    large = np.minimum(large, N_BUCKETS - 1)
    return np.where(dist < max_exact, dist, large).astype(np.int32)


def _bias_by_distance(rel_bias, n):
    buckets = _t5_bucket_np(np.arange(n))
    assert np.all(np.diff(buckets) >= 0)
    runs = np.bincount(buckets, minlength=N_BUCKETS)
    tab = rel_bias.astype(F32).T
    return jnp.concatenate([jnp.broadcast_to(tab[:, k:k + 1], (tab.shape[0], int(runs[k])))
                            for k in range(N_BUCKETS) if runs[k] > 0], axis=1)


def _banded_table(vals):
    hh, n = vals.shape
    period = 4 * BLOCK
    w = jnp.concatenate([jnp.full((hh, BLOCK - 1), NEG, F32), vals,
                         jnp.full((hh, period - (BLOCK - 1) - n), NEG, F32)], axis=1)
    skew = jnp.tile(w, (1, BLOCK + 1))[:, :BLOCK * (period + 1)].reshape(hh, BLOCK, period + 1)
    return jnp.flip(skew[:, :, :2 * BLOCK], axis=2)


def _prompt_bias_tables(pb):
    bias_a = jnp.stack([_banded_table(pb[:N_HEADS_A, 0:BLOCK * dil + 1:dil]) for _, dil in A_BRANCHES])
    bias_b = _banded_table(pb[N_HEADS_A:, 0:WIN_B])
    return bias_a, bias_b.reshape(N_KV_B, GROUP_B, BLOCK, 2 * BLOCK)


DIL_MAX = A_BRANCHES[-1][1]
SA_Y_POS = A_BRANCHES[1][0]
SA_X_M = (WIN_A - SA_Y_POS) // DIL_MAX
SA_X_COLS = SA_X_M * SUBLANES * N_HEADS_A
SA_Y_COLS = SA_Y_POS * N_HEADS_A
SA_NEW_COLS = LANES
SA_COLS = SA_X_COLS + SA_Y_COLS + SA_NEW_COLS
SB_CACHE_COLS = WIN_B * N_KV_B
SB_NEW_COLS = LANES
SB_COLS = SB_CACHE_COLS + SB_NEW_COLS


def _sample_columns(s_new, la, lb):
    m, res, h = np.meshgrid(np.arange(SA_X_M), np.arange(SUBLANES), np.arange(N_HEADS_A), indexing="ij")
    pos_x, head_x = (DIL_MAX * m + res).reshape(-1), h.reshape(-1)
    p, h = np.meshgrid(np.arange(la - SA_Y_POS, la), np.arange(N_HEADS_A), indexing="ij")
    pos_y, head_y = p.reshape(-1), h.reshape(-1)
    h, t = np.meshgrid(np.arange(N_HEADS_A), np.arange(s_new), indexing="ij")
    pad = np.full((SA_NEW_COLS - N_HEADS_A * s_new,), -1)
    pos_a = np.concatenate([pos_x, pos_y, la + t.reshape(-1), pad])
    head_a = np.concatenate([head_x, head_y, h.reshape(-1), pad])
    p, g = np.meshgrid(np.arange(lb), np.arange(N_KV_B), indexing="ij")
    g2, t = np.meshgrid(np.arange(N_KV_B), np.arange(s_new), indexing="ij")
    pad = np.full((SB_NEW_COLS - N_KV_B * s_new,), -1)
    pos_b = np.concatenate([p.reshape(-1), lb + t.reshape(-1), pad])
    head_b = np.concatenate([g.reshape(-1), g2.reshape(-1), pad])
    return pos_a, head_a, pos_b, head_b


def _query_key_bias(pbh, length, s_new):
    n = length + s_new
    rev = jnp.pad(jnp.flip(pbh[:, :n], axis=1), ((0, 0), (0, s_new - 1)))
    return jnp.stack([rev[:, s_new - 1 - i:s_new - 1 - i + n] for i in range(s_new)], axis=1)


def _per_key_head(a, n_key_heads, key_head_of, head_major):
    hh = a.shape[0]
    own = jnp.asarray(np.array([[key_head_of(h) == g for g in range(n_key_heads)] for h in range(hh)]))
    if head_major:
        out = jnp.where(own[:, None, :, None], a[:, :, None, :], NEG)
    else:
        out = jnp.where(own[:, None, None, :], a[:, :, :, None], NEG)
    return out.reshape(hh * a.shape[1], -1)


def _sample_tables(pb, s_new, la, lb):
    assert la == WIN_A and lb == WIN_B and s_new == SUBLANES
    pos_a, head_a, pos_b, head_b = _sample_columns(s_new, la, lb)
    qh, qi = np.meshgrid(np.arange(N_HEADS_A), np.arange(s_new), indexing="ij")
    qh, qi = qh.reshape(-1, 1), qi.reshape(-1, 1)

    dist = la + qi - pos_a[None, :]
    real = (pos_a >= 0)[None, :] & (dist >= 0) & (head_a[None, :] == qh)
    count = np.zeros(dist.shape, np.float32)
    for win, dil in A_BRANCHES:
        count += (real & (dist % dil == 0) & (dist <= win)).astype(np.float32)
    a = _query_key_bias(pb[:N_HEADS_A], la, s_new)
    same = lambda h: h
    a_x = a[:, :, :la - SA_Y_POS].reshape(N_HEADS_A, s_new, SA_X_M, DIL_MAX)[:, :, :, :SUBLANES]
    mb_a = jnp.concatenate([
        _per_key_head(a_x.reshape(N_HEADS_A, s_new, -1), N_HEADS_A, same, False),
        _per_key_head(a[:, :, la - SA_Y_POS:la], N_HEADS_A, same, False),
        _per_key_head(a[:, :, la:], N_HEADS_A, same, True),
        jnp.full((N_HEADS_A * s_new, SA_NEW_COLS - N_HEADS_A * s_new), NEG, F32)], axis=1)
    mb_a = jnp.where(jnp.asarray(count > 0), mb_a, NEG)

    dist_b = lb + qi - pos_b[None, :]
    valid_b = (pos_b >= 0)[None, :] & (dist_b >= 0) & (dist_b < WIN_B) & (head_b[None, :] == qh // GROUP_B)
    b = _query_key_bias(pb[N_HEADS_A:], lb, s_new)
    group = lambda h: h // GROUP_B
    mb_b = jnp.concatenate([
        _per_key_head(b[:, :, :lb], N_KV_B, group, False),
        _per_key_head(b[:, :, lb:], N_KV_B, group, True),
        jnp.full((N_HEADS_B * s_new, SB_NEW_COLS - N_KV_B * s_new), NEG, F32)], axis=1)
    mb_b = jnp.where(jnp.asarray(valid_b), mb_b, NEG)
    return mb_a, jnp.asarray(count), mb_b


def _split_bf16(a):
    hi = a.astype(BF16)
    lo = (a - hi.astype(F32)).astype(BF16)
    return hi, lo


def _mod_kernel(c_ref, w_ref, b_ref, o_ref):
    c = c_ref[...]
    s_hi, s_lo = _split_bf16(c * jax.nn.sigmoid(c))
    w_hi, w_lo = _split_bf16(w_ref[...])
    acc = jnp.dot(s_hi, w_hi, preferred_element_type=F32)
    acc += jnp.dot(s_hi, w_lo, preferred_element_type=F32)
    acc += jnp.dot(s_lo, w_hi, preferred_element_type=F32)
    o_ref[...] = acc + b_ref[...]


def _modulation(c_all, w_ada, b_ada):
    m, d = c_all.shape
    n = w_ada.shape[1]
    tn = 512
    blocks = 2 * (_nbytes((d, tn), F32) + _nbytes((m, tn), F32)) + _nbytes((m, d), F32) * 2
    return pl.pallas_call(
        _mod_kernel,
        grid=(n // tn,),
        in_specs=[pl.BlockSpec((m, d), lambda j: (0, 0)),
                  pl.BlockSpec((d, tn), lambda j: (0, j)),
                  pl.BlockSpec((1, tn), lambda j: (0, j))],
        out_specs=pl.BlockSpec((m, tn), lambda j: (0, j)),
        out_shape=jax.ShapeDtypeStruct((m, n), F32),
        compiler_params=pltpu.CompilerParams(dimension_semantics=("parallel",),
                                             vmem_limit_bytes=_vmem_limit(blocks)),
        name="modulation",
    )(c_all, w_ada, b_ada.reshape(1, n))


def _norm_modulate(x, g, sc, sh):
    ms = jnp.mean(x * x, axis=-1, keepdims=True)
    return (x * lax.rsqrt(ms + EPS) * g) * (1.0 + sc) + sh


NORM_ROWS = 16


def _norm_modulate_rows(x_ref, g_ref, sc_ref, sh_ref, h_scr):
    bb, rb, d = x_ref.shape
    cr = min(rb, NORM_ROWS)
    cs = NORM_ROWS // cr
    for r0 in range(0, bb * rb, NORM_ROWS):
        s, t0 = r0 // rb, r0 % rb
        h = _norm_modulate(x_ref[s:s + cs, t0:t0 + cr, :], g_ref[...], sc_ref[s:s + cs], sh_ref[s:s + cs])
        h_scr[r0:r0 + NORM_ROWS, :] = h.reshape(NORM_ROWS, d).astype(BF16)


def _dot_nt(a, b):
    return lax.dot_general(a, b, (((1,), (1,)), ((), ())), preferred_element_type=F32)


def _proj_kernel(x_ref, sc_ref, sh_ref, g_ref, w_ref, gain_ref, flag_ref, o_ref, h_scr, acc_scr):
    @pl.when(pl.program_id(1) == 0)
    def _():
        _norm_modulate_rows(x_ref, g_ref, sc_ref, sh_ref, h_scr)

    acc_scr[...] = jnp.dot(h_scr[...], w_ref[...], preferred_element_type=F32)
    tm, tn = acc_scr.shape
    for r0 in range(0, tm, 64):
        for c in range(tn // LANES):
            sl = slice(c * LANES, (c + 1) * LANES)
            blk = acc_scr[r0:r0 + 64, sl]
            ms = jnp.mean(blk * blk, axis=-1, keepdims=True)
            nrm = blk * lax.rsqrt(ms + EPS) * gain_ref[:, sl]
            o_ref[r0:r0 + 64, sl] = jnp.where(flag_ref[:, sl] > 0.0, nrm, blk)


def _projection(x3, mod3, g_attn, w_in_bf, gain, flag, bb, rb):
    nb, r, d = x3.shape
    tm = bb * rb
    tn = 512
    n_m = (nb // bb) * (r // rb)
    rblocks = r // rb
    xmap = lambda i, j: (i // rblocks, i % rblocks, 0)
    blocks = (2 * (_nbytes((tm, d), F32) + _nbytes((d, tn), BF16) + _nbytes((tm, tn), F32))
              + _nbytes((tm, d), BF16) + 8 * _nbytes((bb, SUBLANES, d), F32))
    return pl.pallas_call(
        _proj_kernel,
        grid=(n_m, IN_W // tn),
        in_specs=[pl.BlockSpec((bb, rb, d), xmap),
                  pl.BlockSpec((bb, 1, d), lambda i, j: (i // rblocks, 0, 1)),
                  pl.BlockSpec((bb, 1, d), lambda i, j: (i // rblocks, 0, 0)),
                  pl.BlockSpec((1, d), lambda i, j: (0, 0)),
                  pl.BlockSpec((d, tn), lambda i, j: (0, j)),
                  pl.BlockSpec((1, tn), lambda i, j: (0, j)),
                  pl.BlockSpec((1, tn), lambda i, j: (0, j))],
        out_specs=pl.BlockSpec((tm, tn), lambda i, j: (i, j)),
        out_shape=jax.ShapeDtypeStruct((nb * r, IN_W), F32),
        scratch_shapes=[pltpu.VMEM((tm, d), BF16), pltpu.VMEM((tm, tn), F32)],
        compiler_params=pltpu.CompilerParams(dimension_semantics=("parallel", "arbitrary"),
                                             vmem_limit_bytes=_vmem_limit(blocks + _nbytes((tm, tn), F32))),
        name="projection",
    )(x3, mod3, mod3, g_attn.reshape(1, d), w_in_bf, gain, flag)


def _rows(ref, start, dil):
    if dil == 1:
        return ref[start:start + BLOCK, :]
    return ref[pl.ds(start, BLOCK, stride=dil), :]


def _attn_a_prompt_kernel(q_ref, kc_ref, kp_ref, vc_ref, vp_ref, bias_ref, o_ref, num_scr, m_scr, s_scr):
    first_chunk = pl.program_id(0) == 0
    chunk = q_ref.shape[0]
    col = lax.broadcasted_iota(jnp.int32, (BLOCK, 2 * BLOCK), 1)
    for bi, (_, dil) in enumerate(A_BRANCHES):
        nsub = chunk // (BLOCK * dil)
        bias = bias_ref[bi]
        for r in range(dil):
            last = r + dil * BLOCK * (nsub - 1)
            k_prev = _rows(kp_ref, last, dil).astype(BF16)
            v_prev = _rows(vp_ref, last, dil).astype(BF16)
            for ub in range(nsub):
                start = r + dil * BLOCK * ub
                q = _rows(q_ref, start, dil).astype(BF16)
                k_cur = _rows(kc_ref, start, dil).astype(BF16)
                v_cur = _rows(vc_ref, start, dil).astype(BF16)
                z = _dot_nt(q, jnp.concatenate([k_prev, k_cur], axis=0)) * SCALE + bias
                if ub == 0:
                    z = jnp.where(jnp.logical_and(first_chunk, col < BLOCK), NEG, z)
                m = jnp.max(z, axis=-1, keepdims=True)
                e = jnp.exp(z - m)
                s = jnp.sum(e, axis=-1, keepdims=True)
                num = jnp.dot(e.astype(BF16), jnp.concatenate([v_prev, v_cur], axis=0),
                              preferred_element_type=F32)
                if dil == 1:
                    dst = (bi, slice(start, start + BLOCK), slice(None))
                else:
                    dst = (bi, pl.ds(start, BLOCK, stride=dil), slice(None))
                num_scr[dst] = num
                m_scr[dst] = jnp.broadcast_to(m, (BLOCK, LANES))
                s_scr[dst] = jnp.broadcast_to(s, (BLOCK, LANES))
                k_prev, v_prev = k_cur, v_cur
    m_all = jnp.maximum(jnp.maximum(m_scr[0], m_scr[1]), m_scr[2])
    top = jnp.zeros_like(m_all)
    bot = jnp.zeros_like(m_all)
    for bi in range(len(A_BRANCHES)):
        w = jnp.exp(m_scr[bi] - m_all)
        top += w * num_scr[bi]
        bot += w * s_scr[bi]
    o_ref[...] = (top / bot).astype(o_ref.dtype)


def _attn_a_prompt(qkv, bias_a):
    t = qkv.shape[0]
    n_chunks = t // CHUNK_A
    blk = (CHUNK_A, HEAD_DIM)
    prev = lambda b: jnp.maximum(b - 1, 0)
    blocks = (2 * (5 * _nbytes(blk, F32) + _nbytes((3, BLOCK, 2 * BLOCK), F32) + _nbytes(blk, BF16))
              + 9 * _nbytes(blk, F32))
    return pl.pallas_call(
        _attn_a_prompt_kernel,
        grid=(n_chunks, N_HEADS_A),
        in_specs=[pl.BlockSpec(blk, lambda b, h: (b, COL_QA + h)),
                  pl.BlockSpec(blk, lambda b, h: (b, COL_KA + h)),
                  pl.BlockSpec(blk, lambda b, h: (prev(b), COL_KA + h)),
                  pl.BlockSpec(blk, lambda b, h: (b, COL_VA + h)),
                  pl.BlockSpec(blk, lambda b, h: (prev(b), COL_VA + h)),
                  pl.BlockSpec((3, None, BLOCK, 2 * BLOCK), lambda b, h: (0, h, 0, 0))],
        out_specs=pl.BlockSpec(blk, lambda b, h: (b, h)),
        out_shape=jax.ShapeDtypeStruct((t, QA_W), BF16),
        scratch_shapes=[pltpu.VMEM((3, CHUNK_A, HEAD_DIM), F32)] * 3,
        compiler_params=pltpu.CompilerParams(dimension_semantics=("parallel", "parallel"),
                                             vmem_limit_bytes=_vmem_limit(blocks)),
        name="mixer_a_prompt",
    )(qkv, qkv, qkv, qkv, qkv, bias_a)


def _attn_b_prompt_kernel(sink_ref, q_ref, kc_ref, kp_ref, vc_ref, vp_ref, bias_ref, o_ref):
    first_chunk = pl.program_id(0) == 0
    g = pl.program_id(1)
    col = lax.broadcasted_iota(jnp.int32, (BLOCK, 2 * BLOCK), 1)
    k_prev = kp_ref[...].astype(BF16)
    v_prev = vp_ref[...].astype(BF16)
    for blk in range(q_ref.shape[0] // BLOCK):
        rows = slice(blk * BLOCK, (blk + 1) * BLOCK)
        k_cur = kc_ref[rows, :].astype(BF16)
        v_cur = vc_ref[rows, :].astype(BF16)
        kcat = jnp.concatenate([k_prev, k_cur], axis=0)
        vcat = jnp.concatenate([v_prev, v_cur], axis=0)
        for j in range(GROUP_B):
            cols = slice(j * HEAD_DIM, (j + 1) * HEAD_DIM)
            sink = sink_ref[g * GROUP_B + j]
            z = _dot_nt(q_ref[rows, cols].astype(BF16), kcat) * SCALE + bias_ref[j]
            if blk == 0:
                z = jnp.where(jnp.logical_and(first_chunk, col < BLOCK), NEG, z)
            m = jnp.maximum(jnp.max(z, axis=-1, keepdims=True), sink)
            e = jnp.exp(z - m)
            denom = jnp.sum(e, axis=-1, keepdims=True) + jnp.exp(sink - m)
            num = jnp.dot(e.astype(BF16), vcat, preferred_element_type=F32)
            o_ref[rows, cols] = (num / denom).astype(o_ref.dtype)
        k_prev, v_prev = k_cur, v_cur


def _attn_b_prompt(qkv, bias_b, sinks):
    t = qkv.shape[0]
    n_chunks = t // CHUNK_B
    per = CHUNK_B // BLOCK
    qblk = (CHUNK_B, GROUP_B * HEAD_DIM)
    kblk = (CHUNK_B, HEAD_DIM)
    pblk = (BLOCK, HEAD_DIM)
    qcol = COL_QB // GROUP_B
    prev = lambda b: jnp.maximum(b * per - 1, 0)
    blocks = 2 * (_nbytes(qblk, F32) + 2 * _nbytes(kblk, F32) + 2 * _nbytes(pblk, F32)
                  + _nbytes((GROUP_B, BLOCK, 2 * BLOCK), F32) + _nbytes(qblk, BF16))
    return pl.pallas_call(
        _attn_b_prompt_kernel,
        grid=(n_chunks, N_KV_B),
        in_specs=[pl.BlockSpec(memory_space=pltpu.SMEM),
                  pl.BlockSpec(qblk, lambda b, g: (b, qcol + g)),
                  pl.BlockSpec(kblk, lambda b, g: (b, COL_KB + g)),
                  pl.BlockSpec(pblk, lambda b, g: (prev(b), COL_KB + g)),
                  pl.BlockSpec(kblk, lambda b, g: (b, COL_VB + g)),
                  pl.BlockSpec(pblk, lambda b, g: (prev(b), COL_VB + g)),
                  pl.BlockSpec((None, GROUP_B, BLOCK, 2 * BLOCK), lambda b, g: (g, 0, 0, 0))],
        out_specs=pl.BlockSpec(qblk, lambda b, g: (b, g)),
        out_shape=jax.ShapeDtypeStruct((t, QB_W), BF16),
        compiler_params=pltpu.CompilerParams(dimension_semantics=("parallel", "parallel"),
                                             vmem_limit_bytes=_vmem_limit(blocks)),
        name="mixer_b_prompt",
    )(sinks.astype(F32), qkv, qkv, qkv, qkv, qkv, bias_b)


def _heads_to_rows(x, n_heads):
    return jnp.concatenate([x[:, h * HEAD_DIM:(h + 1) * HEAD_DIM] for h in range(n_heads)], axis=0)


def _rows_to_heads(x, n_heads):
    s = x.shape[0] // n_heads
    return jnp.concatenate([x[h * s:(h + 1) * s, :] for h in range(n_heads)], axis=1)


def _attn_sample_kernel(qkv_ref, kx_ref, ky_ref, vx_ref, vy_ref, kb_ref, vb_ref,
                        mba_ref, cnt_ref, mbb_ref, sink_ref, o_ref,
                        ka_scr, va_scr, kb_scr, vb_scr):
    def new_rows(col0, n_heads, n_rows):
        new = _heads_to_rows(qkv_ref[:, col0 * LANES:(col0 + n_heads) * LANES], n_heads)
        return jnp.concatenate([new, jnp.zeros((n_rows - new.shape[0], HEAD_DIM), F32)], axis=0).astype(BF16)

    def fill_a(scr, x_ref, y_ref, col0):
        scr[0:SA_X_COLS, :] = x_ref[...].reshape(SA_X_COLS, HEAD_DIM).astype(BF16)
        scr[SA_X_COLS:SA_X_COLS + SA_Y_COLS, :] = y_ref[...].reshape(SA_Y_COLS, HEAD_DIM).astype(BF16)
        scr[SA_X_COLS + SA_Y_COLS:SA_COLS, :] = new_rows(col0, N_HEADS_A, SA_NEW_COLS)

    def fill_b(scr, c_ref, col0):
        scr[0:SB_CACHE_COLS, :] = c_ref[...].astype(BF16)
        scr[SB_CACHE_COLS:SB_COLS, :] = new_rows(col0, N_KV_B, SB_NEW_COLS)

    fill_a(ka_scr, kx_ref, ky_ref, COL_KA)
    fill_a(va_scr, vx_ref, vy_ref, COL_VA)
    fill_b(kb_scr, kb_ref, COL_KB)
    fill_b(vb_scr, vb_ref, COL_VB)

    qa = _heads_to_rows(qkv_ref[:, COL_QA * LANES:COL_QA * LANES + QA_W], N_HEADS_A)
    z = _dot_nt(qa.astype(BF16), ka_scr[...]) * SCALE + mba_ref[...]
    m = jnp.max(z, axis=-1, keepdims=True)
    p = cnt_ref[...] * jnp.exp(z - m)
    denom = jnp.sum(p, axis=-1, keepdims=True)
    oa = jnp.dot(p.astype(BF16), va_scr[...], preferred_element_type=F32) / denom
    o_ref[:, 0:QA_W] = _rows_to_heads(oa, N_HEADS_A)

    qb = _heads_to_rows(qkv_ref[:, COL_QB * LANES:COL_QB * LANES + QB_W], N_HEADS_B)
    zb = _dot_nt(qb.astype(BF16), kb_scr[...]) * SCALE + mbb_ref[...]
    sink = sink_ref[:, 0:1]
    mb = jnp.maximum(jnp.max(zb, axis=-1, keepdims=True), sink)
    eb = jnp.exp(zb - mb)
    denom_b = jnp.sum(eb, axis=-1, keepdims=True) + jnp.exp(sink - mb)
    ob = jnp.dot(eb.astype(BF16), vb_scr[...], preferred_element_type=F32) / denom_b
    o_ref[:, QA_W:QA_W + QB_W] = _rows_to_heads(ob, N_HEADS_B)


def _attn_sample(qkv, layer, cache_ak, cache_av, cache_bk, cache_bv, mb_a, cnt_a, mb_b, sink_rows, s_new):
    depth, db, la = cache_ak.shape[0], cache_ak.shape[1], cache_ak.shape[2]
    lb = cache_bk.shape[2]
    ax_k = cache_ak.reshape(depth, db, la // DIL_MAX, DIL_MAX, N_HEADS_A, HEAD_DIM)
    ax_v = cache_av.reshape(depth, db, la // DIL_MAX, DIL_MAX, N_HEADS_A, HEAD_DIM)
    bk = cache_bk.reshape(depth, db, lb * N_KV_B, HEAD_DIM)
    bv = cache_bv.reshape(depth, db, lb * N_KV_B, HEAD_DIM)
    xblk = (None, None, SA_X_M, SUBLANES, N_HEADS_A, HEAD_DIM)
    yblk = (None, None, SA_Y_POS, N_HEADS_A, HEAD_DIM)
    bblk = (None, None, SB_CACHE_COLS, HEAD_DIM)
    y_idx = (la - SA_Y_POS) // SA_Y_POS
    rows_q = N_HEADS_A * s_new
    const = lambda b: (0, 0)
    blocks = (2 * (_nbytes((s_new, IN_W), F32) + 2 * _nbytes((SA_X_COLS + SA_Y_COLS, HEAD_DIM), F32)
                   + 2 * _nbytes((SB_CACHE_COLS, HEAD_DIM), F32)
                   + 2 * _nbytes((rows_q, SA_COLS), F32) + _nbytes((rows_q, SB_COLS), F32)
                   + _nbytes((rows_q, LANES), F32) + _nbytes((s_new, QA_W + QB_W), F32))
              + 2 * _nbytes((SA_COLS, HEAD_DIM), BF16) + 2 * _nbytes((SB_COLS, HEAD_DIM), BF16)
              + 4 * _nbytes((rows_q, SA_COLS), F32))
    return pl.pallas_call(
        _attn_sample_kernel,
        grid=(db,),
        in_specs=[pl.BlockSpec((s_new, IN_W), lambda b: (b, 0)),
                  pl.BlockSpec(xblk, lambda b: (layer, b, 0, 0, 0, 0)),
                  pl.BlockSpec(yblk, lambda b: (layer, b, y_idx, 0, 0)),
                  pl.BlockSpec(xblk, lambda b: (layer, b, 0, 0, 0, 0)),
                  pl.BlockSpec(yblk, lambda b: (layer, b, y_idx, 0, 0)),
                  pl.BlockSpec(bblk, lambda b: (layer, b, 0, 0)),
                  pl.BlockSpec(bblk, lambda b: (layer, b, 0, 0)),
                  pl.BlockSpec((rows_q, SA_COLS), const),
                  pl.BlockSpec((rows_q, SA_COLS), const),
                  pl.BlockSpec((rows_q, SB_COLS), const),
                  pl.BlockSpec((rows_q, LANES), const)],
        out_specs=pl.BlockSpec((s_new, QA_W + QB_W), lambda b: (b, 0)),
        out_shape=jax.ShapeDtypeStruct((db * s_new, QA_W + QB_W), F32),
        scratch_shapes=[pltpu.VMEM((SA_COLS, HEAD_DIM), BF16), pltpu.VMEM((SA_COLS, HEAD_DIM), BF16),
                        pltpu.VMEM((SB_COLS, HEAD_DIM), BF16), pltpu.VMEM((SB_COLS, HEAD_DIM), BF16)],
        compiler_params=pltpu.CompilerParams(dimension_semantics=("parallel",),
                                             vmem_limit_bytes=_vmem_limit(blocks)),
        name="mixers_sample",
    )(qkv, ax_k, cache_ak, ax_v, cache_av, bk, bv, mb_a, cnt_a, mb_b, sink_rows)


def _outproj_kernel(x_ref, ma_ref, mb_ref, wa_ref, wb_ref, gt_ref, o_ref):
    acc = jnp.dot(ma_ref[...].astype(BF16), wa_ref[...], preferred_element_type=F32)
    acc += jnp.dot(mb_ref[...].astype(BF16), wb_ref[...], preferred_element_type=F32)
    o_ref[...] = x_ref[...] + gt_ref[...] * acc.reshape(o_ref.shape)


def _out_projection(x3, mix_a, mix_b, col_a, col_b, w_out_bf, mod3, bb, rb):
    nb, r, d = x3.shape
    tm = bb * rb
    tn = 1024
    rblocks = r // rb
    n_m = (nb // bb) * rblocks
    kh = QA_W
    blocks = 2 * (2 * _nbytes((tm, tn), F32) + _nbytes((tm, kh), mix_a.dtype) + _nbytes((tm, kh), mix_b.dtype)
                  + 2 * _nbytes((kh, tn), BF16) + _nbytes((bb, SUBLANES, tn), F32))
    return pl.pallas_call(
        _outproj_kernel,
        grid=(n_m, d // tn),
        in_specs=[pl.BlockSpec((bb, rb, tn), lambda i, j: (i // rblocks, i % rblocks, j)),
                  pl.BlockSpec((tm, kh), lambda i, j: (i, col_a)),
                  pl.BlockSpec((tm, kh), lambda i, j: (i, col_b)),
                  pl.BlockSpec((kh, tn), lambda i, j: (0, j)),
                  pl.BlockSpec((kh, tn), lambda i, j: (1, j)),
                  pl.BlockSpec((bb, 1, tn), lambda i, j: (i // rblocks, 0, 2 * (d // tn) + j))],
        out_specs=pl.BlockSpec((bb, rb, tn), lambda i, j: (i // rblocks, i % rblocks, j)),
        out_shape=jax.ShapeDtypeStruct(x3.shape, F32),
        compiler_params=pltpu.CompilerParams(dimension_semantics=("parallel", "parallel"),
                                             vmem_limit_bytes=_vmem_limit(blocks)),
        name="out_projection",
    )(x3, mix_a, mix_b, w_out_bf, w_out_bf, mod3)


def _ffn_kernel(x_ref, sc_ref, sh_ref, gt_ref, g_ref, wg_ref, wu_ref, wd_ref, wc_ref, bc_ref, hist_ref,
                o_ref, cs_ref, h_scr, gate_scr, up_scr, act_scr, carry_scr, *, rblocks, chunk_seqs, chunk_rows):
    i = pl.program_id(0)
    f = pl.program_id(1)
    n_f = pl.num_programs(1)
    bb, rb, d = x_ref.shape
    tm, tf = bb * rb, wg_ref.shape[1]

    @pl.when(f == 0)
    def _():
        _norm_modulate_rows(x_ref, g_ref, sc_ref, sh_ref, h_scr)
        o_ref[...] = jnp.zeros(o_ref.shape, F32)

    if rblocks > 1:
        @pl.when(i % rblocks == 0)
        def _():
            carry_scr[f] = hist_ref[...]
        history = lambda s0, n: carry_scr[f]
    else:
        history = lambda s0, n: hist_ref[s0:s0 + n]

    h = h_scr[...]
    gate_scr[0:SUBLANES, :] = jnp.zeros((SUBLANES, tf), F32)
    gate_scr[SUBLANES:, :] = jnp.dot(h, wg_ref[...], preferred_element_type=F32)
    up_scr[...] = jnp.dot(h, wu_ref[...], preferred_element_type=F32)

    n_rows = chunk_seqs * chunk_rows
    shape = (chunk_seqs, chunk_rows, tf)
    t = lax.broadcasted_iota(jnp.int32, shape, 1)
    for r0 in range(0, tm, n_rows):
        def shifted(s):
            lo = SUBLANES + r0 - s
            return gate_scr[lo:lo + n_rows, :].reshape(shape)
        g0, g1, g2 = shifted(0), shifted(1), shifted(2)
        if r0 % rb == 0:
            hist = history(r0 // rb, chunk_seqs)
            g1 = jnp.where(t >= 1, g1, hist[:, 1:2, :])
            g2 = jnp.where(t >= 2, g2, jnp.where(t == 0, hist[:, 0:1, :], hist[:, 1:2, :]))
        gc = bc_ref[...] + wc_ref[0:1, :] * g2 + wc_ref[1:2, :] * g1 + wc_ref[2:3, :] * g0
        act = jax.nn.gelu(gc, approximate=True) * up_scr[r0:r0 + n_rows, :].reshape(shape)
        act_scr[r0:r0 + n_rows, :] = act.reshape(n_rows, tf).astype(BF16)
        if (r0 + n_rows) % rb == 0:
            s0 = (r0 + n_rows) // rb - chunk_seqs
            tail = g0[:, chunk_rows - (CONV_W - 1):, :]
            cs_ref[s0:s0 + chunk_seqs] = tail
            if rblocks > 1:
                carry_scr[f] = tail
    o_ref[...] += jnp.dot(act_scr[...], wd_ref[...], preferred_element_type=F32).reshape(bb, rb, d)

    @pl.when(f == n_f - 1)
    def _():
        o_ref[...] = x_ref[...] + gt_ref[...] * o_ref[...]


def _conv_ffn(x3, mod3, g_ffn, wg_bf, wu_bf, wd_bf, w_conv, b_conv, hist, bb, rb):
    nb, r, d = x3.shape
    f_pad = wg_bf.shape[1]
    tf = FF_TILE
    n_f = f_pad // tf
    tm = bb * rb
    rblocks = r // rb
    assert rblocks == 1 or bb == 1
    n_m = (nb // bb) * rblocks
    carry_slots = n_f if rblocks > 1 else 1
    chunk_rows = min(rb, 64)
    chunk_seqs = 64 // chunk_rows
    assert rb % chunk_rows == 0 and bb % chunk_seqs == 0 and chunk_rows >= CONV_W - 1
    modmap = lambda k: (lambda i, f: (i // rblocks, 0, k))
    blocks = (2 * (2 * _nbytes((tm, d), F32) + 2 * _nbytes((d, tf), BF16) + _nbytes((tf, d), BF16)
                   + 2 * _nbytes((bb, SUBLANES, tf), F32) + 3 * _nbytes((bb, SUBLANES, d), F32))
              + _nbytes((tm, d), BF16) + carry_slots * _nbytes((bb, SUBLANES, tf), F32)
              + 3 * _nbytes((tm + SUBLANES, tf), F32))
    y, tails = pl.pallas_call(
        functools.partial(_ffn_kernel, rblocks=rblocks, chunk_seqs=chunk_seqs, chunk_rows=chunk_rows),
        grid=(n_m, n_f),
        in_specs=[pl.BlockSpec((bb, rb, d), lambda i, f: (i // rblocks, i % rblocks, 0)),
                  pl.BlockSpec((bb, 1, d), modmap(4)),
                  pl.BlockSpec((bb, 1, d), modmap(3)),
                  pl.BlockSpec((bb, 1, d), modmap(5)),
                  pl.BlockSpec((1, d), lambda i, f: (0, 0)),
                  pl.BlockSpec((d, tf), lambda i, f: (0, f)),
                  pl.BlockSpec((d, tf), lambda i, f: (0, f)),
                  pl.BlockSpec((tf, d), lambda i, f: (f, 0)),
                  pl.BlockSpec((CONV_W, tf), lambda i, f: (0, f)),
                  pl.BlockSpec((1, tf), lambda i, f: (0, f)),
                  pl.BlockSpec((bb, CONV_W - 1, tf), lambda i, f: (i // rblocks, 0, f))],
        out_specs=[pl.BlockSpec((bb, rb, d), lambda i, f: (i // rblocks, i % rblocks, 0)),
                   pl.BlockSpec((bb, CONV_W - 1, tf), lambda i, f: (i, 0, f))],
        out_shape=[jax.ShapeDtypeStruct(x3.shape, F32),
                   jax.ShapeDtypeStruct((n_m * bb, CONV_W - 1, f_pad), F32)],
        scratch_shapes=[pltpu.VMEM((tm, d), BF16),
                        pltpu.VMEM((tm + SUBLANES, tf), F32),
                        pltpu.VMEM((tm, tf), F32),
                        pltpu.VMEM((tm, tf), BF16),
                        pltpu.VMEM((carry_slots, bb, CONV_W - 1, tf), F32)],
        compiler_params=pltpu.CompilerParams(dimension_semantics=("arbitrary", "arbitrary"),
                                             vmem_limit_bytes=_vmem_limit(blocks)),
        name="conv_ffn",
    )(x3, mod3, mod3, mod3, g_ffn.reshape(1, d), wg_bf, wu_bf, wd_bf, w_conv, b_conv.reshape(1, f_pad), hist)
    state = tails.reshape(nb // bb, rblocks, bb, CONV_W - 1, f_pad)[:, rblocks - 1]
    return y, state.reshape(nb, CONV_W - 1, f_pad)


def _pad_last(a, n):
    return jnp.pad(a, [(0, 0)] * (a.ndim - 1) + [(0, n - a.shape[-1])])


def kernel(x_prompt, x_sample, c_prompt, c_sample, cache_a_k, cache_a_v, cache_b_k, cache_b_v, state_conv, rel_bias, w_ada, b_ada, g_attn, g_ffn, w_in, g_qn_a, g_kn_a, g_qn_b, g_kn_b, sinks, w_out, w_gate, w_up, w_conv, b_conv, w_down):
    depth = w_in.shape[0]
    batch, seq, d = x_prompt.shape
    db, s_new, _ = x_sample.shape
    d_ff = w_gate.shape[2]
    f_pad = -(-d_ff // FF_TILE) * FF_TILE
    la, lb = cache_a_k.shape[2], cache_b_k.shape[2]
    assert batch == 1 and seq % CHUNK_A == 0 and s_new == SUBLANES

    pb = _bias_by_distance(rel_bias, max(WIN_A + 1, la + s_new))
    bias_a, bias_b = _prompt_bias_tables(pb)
    mb_a, cnt_a, mb_b = _sample_tables(pb, s_new, la, lb)

    c_all = jnp.concatenate([c_prompt, jnp.zeros((SUBLANES - batch, d), F32), c_sample], axis=0)

    ones = jnp.ones((HEAD_DIM,), F32)
    flag = jnp.concatenate([jnp.ones((2 * QA_W,), F32), jnp.zeros((QA_W,), F32), jnp.ones((QB_W + KB_W,), F32),
                            jnp.zeros((KB_W,), F32)]).reshape(1, IN_W)

    xp = x_prompt
    xs = x_sample
    hist_p = jnp.zeros((batch, CONV_W - 1, f_pad), F32)
    outs = [[] for _ in range(10)]
    for l in range(depth):
        w_in_bf = w_in[l].astype(BF16)
        w_out_bf = w_out[l].astype(BF16)
        wg_bf = _pad_last(w_gate[l], f_pad).astype(BF16)
        wu_bf = _pad_last(w_up[l], f_pad).astype(BF16)
        wd_bf = jnp.pad(w_down[l], ((0, f_pad - d_ff), (0, 0))).astype(BF16)
        wc = _pad_last(w_conv[l], f_pad)
        bc = _pad_last(b_conv[l], f_pad)
        gain = jnp.concatenate([jnp.tile(g_qn_a[l], N_HEADS_A), jnp.tile(g_kn_a[l], N_HEADS_A),
                                jnp.tile(ones, N_HEADS_A), jnp.tile(g_qn_b[l], N_HEADS_B),
                                jnp.tile(g_kn_b[l], N_KV_B), jnp.tile(ones, N_KV_B)]).reshape(1, IN_W)
        sink_rows = jnp.broadcast_to(jnp.repeat(sinks[l].astype(F32), s_new)[:, None], (N_HEADS_B * s_new, LANES))

        mod = _modulation(c_all, w_ada[l], b_ada[l])
        mod_p = mod[0:batch].reshape(batch, 1, 6 * d)
        mod_s = mod[SUBLANES:].reshape(db, 1, 6 * d)

        qkv_p = _projection(xp, mod_p, g_attn[l], w_in_bf, gain, flag, 1, 1024)
        mix_pa = _attn_a_prompt(qkv_p, bias_a)
        mix_pb = _attn_b_prompt(qkv_p, bias_b, sinks[l])
        xp = _out_projection(xp, mix_pa, mix_pb, 0, 0, w_out_bf, mod_p, 1, 1024)
        xp, conv_p = _conv_ffn(xp, mod_p, g_ffn[l], wg_bf, wu_bf, wd_bf, wc, bc, hist_p, 1, 512)
        keep_a, keep_b = min(WIN_A, seq), min(WIN_B, seq)
        outs[0].append(qkv_p[seq - keep_a:, COL_KA * LANES:COL_KA * LANES + QA_W].reshape(batch, keep_a, N_HEADS_A, HEAD_DIM))
        outs[1].append(qkv_p[seq - keep_a:, COL_VA * LANES:COL_VA * LANES + QA_W].reshape(batch, keep_a, N_HEADS_A, HEAD_DIM))
        outs[2].append(qkv_p[seq - keep_b:, COL_KB * LANES:COL_KB * LANES + KB_W].reshape(batch, keep_b, N_KV_B, HEAD_DIM))
        outs[3].append(qkv_p[seq - keep_b:, COL_VB * LANES:COL_VB * LANES + KB_W].reshape(batch, keep_b, N_KV_B, HEAD_DIM))
        outs[4].append(conv_p[:, :, :d_ff])

        qkv_s = _projection(xs, mod_s, g_attn[l], w_in_bf, gain, flag, db, s_new)
        mix_s = _attn_sample(qkv_s, l, cache_a_k, cache_a_v, cache_b_k, cache_b_v,
                             mb_a, cnt_a, mb_b, sink_rows, s_new)
        xs = _out_projection(xs, mix_s, mix_s, 0, 1, w_out_bf, mod_s, db, s_new)
        xs, conv_s = _conv_ffn(xs, mod_s, g_ffn[l], wg_bf, wu_bf, wd_bf, wc, bc,
                               _pad_last(state_conv[l], f_pad), db // 2, s_new)
        outs[5].append(qkv_s[:, COL_KA * LANES:COL_KA * LANES + QA_W].reshape(db, s_new, N_HEADS_A, HEAD_DIM))
        outs[6].append(qkv_s[:, COL_VA * LANES:COL_VA * LANES + QA_W].reshape(db, s_new, N_HEADS_A, HEAD_DIM))
        outs[7].append(qkv_s[:, COL_KB * LANES:COL_KB * LANES + KB_W].reshape(db, s_new, N_KV_B, HEAD_DIM))
        outs[8].append(qkv_s[:, COL_VB * LANES:COL_VB * LANES + KB_W].reshape(db, s_new, N_KV_B, HEAD_DIM))
        outs[9].append(conv_s[:, :, :d_ff])

    return (xp, xs) + tuple(jnp.stack(o) for o in outs)
```

```python
import functools
import math

import numpy as np
import jax
import jax.numpy as jnp
from jax import lax
from jax.experimental import pallas as pl
from jax.experimental.pallas import tpu as pltpu

F32 = jnp.float32
BF16 = jnp.bfloat16

LANES = 128
SUBLANES = 8
VMEM_BYTES_V7X = 64 * 1024 * 1024

HEAD_DIM = 128
N_HEADS_A = 8
N_HEADS_B = 8
N_KV_B = 2
GROUP_B = N_HEADS_B // N_KV_B
A_BRANCHES = ((128, 1), (512, 4), (2048, 16))
WIN_A = 2048
WIN_B = 128
BLOCK = 128
N_BUCKETS = 32
MAX_DISTANCE = 2048
CONV_W = 3
EPS = 1e-6
NEG = -1e30
SCALE = HEAD_DIM ** -0.5
LOG2E = math.log2(math.e)
QK_SCALE_LOG2 = SCALE * LOG2E

QA_W = N_HEADS_A * HEAD_DIM
QB_W = N_HEADS_B * HEAD_DIM
KB_W = N_KV_B * HEAD_DIM
IN_W = 3 * QA_W + QB_W + 2 * KB_W
COL_QA, COL_KA, COL_VA = 0, QA_W // LANES, 2 * QA_W // LANES
COL_QB = 3 * QA_W // LANES
COL_KB = COL_QB + QB_W // LANES
COL_VB = COL_KB + KB_W // LANES

CHUNK_A = WIN_A
CHUNK_B = 1024
FF_TILE = 512


def _vmem_limit(block_bytes):
    return int(min(VMEM_BYTES_V7X - (4 << 20), block_bytes + (12 << 20)))


def _nbytes(shape, dtype):
    return int(np.prod(shape)) * jnp.dtype(dtype).itemsize


def _t5_bucket_np(dist):
    dist = np.maximum(dist, 0)
    max_exact = N_BUCKETS // 2
    ratio = np.log(np.maximum(dist, 1).astype(np.float64) / max_exact) / math.log(MAX_DISTANCE / max_exact)
    large = max_exact + (ratio * (N_BUCKETS - max_exact)).astype(np.int32)
    large = np.minimum(large, N_BUCKETS - 1)
    return np.where(dist < max_exact, dist, large).astype(np.int32)


def _bias_by_distance(rel_bias, n):
    buckets = _t5_bucket_np(np.arange(n))
    assert np.all(np.diff(buckets) >= 0)
    runs = np.bincount(buckets, minlength=N_BUCKETS)
    tab = rel_bias.astype(F32).T
    return jnp.concatenate([jnp.broadcast_to(tab[:, k:k + 1], (tab.shape[0], int(runs[k])))
                            for k in range(N_BUCKETS) if runs[k] > 0], axis=1)


def _banded_table(vals):
    hh, n = vals.shape
    period = 4 * BLOCK
    w = jnp.concatenate([jnp.full((hh, BLOCK - 1), NEG, F32), vals,
                         jnp.full((hh, period - (BLOCK - 1) - n), NEG, F32)], axis=1)
    skew = jnp.tile(w, (1, BLOCK + 1))[:, :BLOCK * (period + 1)].reshape(hh, BLOCK, period + 1)
    return jnp.flip(skew[:, :, :2 * BLOCK], axis=2)


def _prompt_bias_tables(pb):
    bias_a = jnp.stack([_banded_table(pb[:N_HEADS_A, 0:BLOCK * dil + 1:dil]) for _, dil in A_BRANCHES])
    bias_b = _banded_table(pb[N_HEADS_A:, 0:WIN_B])
    return bias_a, bias_b.reshape(N_KV_B, GROUP_B, BLOCK, 2 * BLOCK)


DIL_MAX = A_BRANCHES[-1][1]
SA_Y_POS = A_BRANCHES[1][0]
SA_X_M = (WIN_A - SA_Y_POS) // DIL_MAX
SA_X_COLS = SA_X_M * SUBLANES * N_HEADS_A
SA_Y_COLS = SA_Y_POS * N_HEADS_A
SA_NEW_COLS = LANES
SA_COLS = SA_X_COLS + SA_Y_COLS + SA_NEW_COLS
SB_CACHE_COLS = WIN_B * N_KV_B
SB_NEW_COLS = LANES
SB_COLS = SB_CACHE_COLS + SB_NEW_COLS


def _sample_columns(s_new, la, lb):
    m, res, h = np.meshgrid(np.arange(SA_X_M), np.arange(SUBLANES), np.arange(N_HEADS_A), indexing="ij")
    pos_x, head_x = (DIL_MAX * m + res).reshape(-1), h.reshape(-1)
    p, h = np.meshgrid(np.arange(la - SA_Y_POS, la), np.arange(N_HEADS_A), indexing="ij")
    pos_y, head_y = p.reshape(-1), h.reshape(-1)
    h, t = np.meshgrid(np.arange(N_HEADS_A), np.arange(s_new), indexing="ij")
    pad = np.full((SA_NEW_COLS - N_HEADS_A * s_new,), -1)
    pos_a = np.concatenate([pos_x, pos_y, la + t.reshape(-1), pad])
    head_a = np.concatenate([head_x, head_y, h.reshape(-1), pad])
    p, g = np.meshgrid(np.arange(lb), np.arange(N_KV_B), indexing="ij")
    g2, t = np.meshgrid(np.arange(N_KV_B), np.arange(s_new), indexing="ij")
    pad = np.full((SB_NEW_COLS - N_KV_B * s_new,), -1)
    pos_b = np.concatenate([p.reshape(-1), lb + t.reshape(-1), pad])
    head_b = np.concatenate([g.reshape(-1), g2.reshape(-1), pad])
    return pos_a, head_a, pos_b, head_b


def _query_key_bias(pbh, length, s_new):
    n = length + s_new
    rev = jnp.pad(jnp.flip(pbh[:, :n], axis=1), ((0, 0), (0, s_new - 1)))
    return jnp.stack([rev[:, s_new - 1 - i:s_new - 1 - i + n] for i in range(s_new)], axis=1)


def _per_key_head(a, n_key_heads, key_head_of, head_major):
    hh = a.shape[0]
    own = jnp.asarray(np.array([[key_head_of(h) == g for g in range(n_key_heads)] for h in range(hh)]))
    if head_major:
        out = jnp.where(own[:, None, :, None], a[:, :, None, :], NEG)
    else:
        out = jnp.where(own[:, None, None, :], a[:, :, :, None], NEG)
    return out.reshape(hh * a.shape[1], -1)


def _sample_tables(pb, s_new, la, lb):
    assert la == WIN_A and lb == WIN_B and s_new == SUBLANES
    pos_a, head_a, pos_b, head_b = _sample_columns(s_new, la, lb)
    qh, qi = np.meshgrid(np.arange(N_HEADS_A), np.arange(s_new), indexing="ij")
    qh, qi = qh.reshape(-1, 1), qi.reshape(-1, 1)

    dist = la + qi - pos_a[None, :]
    real = (pos_a >= 0)[None, :] & (dist >= 0) & (head_a[None, :] == qh)
    count = np.zeros(dist.shape, np.float32)
    for win, dil in A_BRANCHES:
        count += (real & (dist % dil == 0) & (dist <= win)).astype(np.float32)
    a = _query_key_bias(pb[:N_HEADS_A], la, s_new)
    same = lambda h: h
    a_x = a[:, :, :la - SA_Y_POS].reshape(N_HEADS_A, s_new, SA_X_M, DIL_MAX)[:, :, :, :SUBLANES]
    mb_a = jnp.concatenate([
        _per_key_head(a_x.reshape(N_HEADS_A, s_new, -1), N_HEADS_A, same, False),
        _per_key_head(a[:, :, la - SA_Y_POS:la], N_HEADS_A, same, False),
        _per_key_head(a[:, :, la:], N_HEADS_A, same, True),
        jnp.full((N_HEADS_A * s_new, SA_NEW_COLS - N_HEADS_A * s_new), NEG, F32)], axis=1)
    mb_a = jnp.where(jnp.asarray(count > 0), mb_a + jnp.asarray(np.log2(np.maximum(count, 1.0))), NEG)

    dist_b = lb + qi - pos_b[None, :]
    valid_b = (pos_b >= 0)[None, :] & (dist_b >= 0) & (dist_b < WIN_B) & (head_b[None, :] == qh // GROUP_B)
    b = _query_key_bias(pb[N_HEADS_A:], lb, s_new)
    group = lambda h: h // GROUP_B
    mb_b = jnp.concatenate([
        _per_key_head(b[:, :, :lb], N_KV_B, group, False),
        _per_key_head(b[:, :, lb:], N_KV_B, group, True),
        jnp.full((N_HEADS_B * s_new, SB_NEW_COLS - N_KV_B * s_new), NEG, F32)], axis=1)
    mb_b = jnp.where(jnp.asarray(valid_b), mb_b, NEG)
    return mb_a, mb_b


def _split_bf16(a):
    hi = a.astype(BF16)
    lo = (a - hi.astype(F32)).astype(BF16)
    return hi, lo


def _mod_kernel(c_ref, w_ref, b_ref, o_ref):
    c = c_ref[...]
    s_hi, s_lo = _split_bf16(c * jax.nn.sigmoid(c))
    w_hi, w_lo = _split_bf16(w_ref[...])
    acc = jnp.dot(s_hi, w_hi, preferred_element_type=F32)
    acc += jnp.dot(s_hi, w_lo, preferred_element_type=F32)
    acc += jnp.dot(s_lo, w_hi, preferred_element_type=F32)
    o_ref[...] = acc + b_ref[...]


def _modulation(c_all, w_ada, b_ada):
    m, d = c_all.shape
    depth, _, n = w_ada.shape
    tn = 512
    blocks = 2 * (_nbytes((d, tn), F32) + _nbytes((m, tn), F32)) + _nbytes((m, d), F32) * 2
    return pl.pallas_call(
        _mod_kernel,
        grid=(depth, n // tn),
        in_specs=[pl.BlockSpec((m, d), lambda l, j: (0, 0)),
                  pl.BlockSpec((None, d, tn), lambda l, j: (l, 0, j)),
                  pl.BlockSpec((None, 1, tn), lambda l, j: (l, 0, j))],
        out_specs=pl.BlockSpec((None, m, tn), lambda l, j: (l, 0, j)),
        out_shape=jax.ShapeDtypeStruct((depth, m, n), F32),
        compiler_params=pltpu.CompilerParams(dimension_semantics=("parallel", "parallel"),
                                             vmem_limit_bytes=_vmem_limit(blocks)),
        name="modulation",
    )(c_all, w_ada, b_ada.reshape(depth, 1, n))


def _norm_modulate(x, g, sc, sh):
    ms = jnp.mean(x * x, axis=-1, keepdims=True)
    return (x * lax.rsqrt(ms + EPS) * g) * (1.0 + sc) + sh


NORM_ROWS = 16


def _norm_modulate_rows(x_ref, g_ref, sc_ref, sh_ref, h_scr):
    bb, rb, d = x_ref.shape
    cr = min(rb, NORM_ROWS)
    cs = NORM_ROWS // cr
    for r0 in range(0, bb * rb, NORM_ROWS):
        s, t0 = r0 // rb, r0 % rb
        h = _norm_modulate(x_ref[s:s + cs, t0:t0 + cr, :], g_ref[...], sc_ref[s:s + cs], sh_ref[s:s + cs])
        h_scr[r0:r0 + NORM_ROWS, :] = h.reshape(NORM_ROWS, d).astype(BF16)


def _dot_nt(a, b):
    return lax.dot_general(a, b, (((1,), (1,)), ((), ())), preferred_element_type=F32)


def _proj_kernel(x_ref, sc_ref, sh_ref, g_ref, w_ref, gain_ref, flag_ref, o_ref, h_scr, acc_scr):
    @pl.when(pl.program_id(1) == 0)
    def _():
        _norm_modulate_rows(x_ref, g_ref, sc_ref, sh_ref, h_scr)

    acc_scr[...] = jnp.dot(h_scr[...], w_ref[...], preferred_element_type=F32)
    tm, tn = acc_scr.shape
    for r0 in range(0, tm, 64):
        for c in range(tn // LANES):
            sl = slice(c * LANES, (c + 1) * LANES)
            blk = acc_scr[r0:r0 + 64, sl]
            ms = jnp.mean(blk * blk, axis=-1, keepdims=True)
            nrm = blk * lax.rsqrt(ms + EPS) * gain_ref[:, sl]
            o_ref[r0:r0 + 64, sl] = jnp.where(flag_ref[:, sl] > 0.0, nrm, blk)


def _projection(x3, mod3, layer, g_attn, w_in_bf, gain, flag, bb, rb):
    nb, r, d = x3.shape
    tm = bb * rb
    tn = 512
    n_m = (nb // bb) * (r // rb)
    rblocks = r // rb
    xmap = lambda i, j: (i // rblocks, i % rblocks, 0)
    blocks = (2 * (_nbytes((tm, d), F32) + _nbytes((d, tn), BF16) + _nbytes((tm, tn), F32))
              + _nbytes((tm, d), BF16) + 8 * _nbytes((bb, SUBLANES, d), F32) + _nbytes((tm, tn), F32))
    return pl.pallas_call(
        _proj_kernel,
        grid=(n_m, IN_W // tn),
        in_specs=[pl.BlockSpec((bb, rb, d), xmap),
                  pl.BlockSpec((bb, 1, d), lambda i, j: (i // rblocks, 0, 1)),
                  pl.BlockSpec((bb, 1, d), lambda i, j: (i // rblocks, 0, 0)),
                  pl.BlockSpec((1, d), lambda i, j: (0, 0)),
                  pl.BlockSpec((None, d, tn), lambda i, j: (layer, 0, j)),
                  pl.BlockSpec((1, tn), lambda i, j: (0, j)),
                  pl.BlockSpec((1, tn), lambda i, j: (0, j))],
        out_specs=pl.BlockSpec((tm, tn), lambda i, j: (i, j)),
        out_shape=jax.ShapeDtypeStruct((nb * r, IN_W), F32),
        scratch_shapes=[pltpu.VMEM((tm, d), BF16), pltpu.VMEM((tm, tn), F32)],
        compiler_params=pltpu.CompilerParams(dimension_semantics=("parallel", "arbitrary"),
                                             vmem_limit_bytes=_vmem_limit(blocks)),
        name="projection",
    )(x3, mod3, mod3, g_attn.reshape(1, d), w_in_bf, gain, flag)


def _rows(ref, start, dil):
    if dil == 1:
        return ref[start:start + BLOCK, :]
    return ref[pl.ds(start, BLOCK, stride=dil), :]


def _attn_a_prompt_kernel(q_ref, kc_ref, kp_ref, vc_ref, vp_ref, bias_ref, o_ref, num_scr, m_scr, s_scr):
    first_chunk = pl.program_id(0) == 0
    chunk = q_ref.shape[0]
    col = lax.broadcasted_iota(jnp.int32, (BLOCK, 2 * BLOCK), 1)
    ones = jnp.ones((2 * BLOCK, LANES), BF16)
    for bi, (_, dil) in enumerate(A_BRANCHES):
        nsub = chunk // (BLOCK * dil)
        bias = bias_ref[bi]
        for r in range(dil):
            last = r + dil * BLOCK * (nsub - 1)
            k_prev = _rows(kp_ref, last, dil).astype(BF16)
            v_prev = _rows(vp_ref, last, dil).astype(BF16)
            for ub in range(nsub):
                start = r + dil * BLOCK * ub
                q = (_rows(q_ref, start, dil) * QK_SCALE_LOG2).astype(BF16)
                k_cur = _rows(kc_ref, start, dil).astype(BF16)
                v_cur = _rows(vc_ref, start, dil).astype(BF16)
                z = _dot_nt(q, jnp.concatenate([k_prev, k_cur], axis=0)) + bias
                if ub == 0:
                    z = jnp.where(jnp.logical_and(first_chunk, col < BLOCK), NEG, z)
                m = jnp.max(z, axis=-1, keepdims=True)
                e = jnp.exp2(z - m).astype(BF16)
                v_ext = jnp.concatenate([jnp.concatenate([v_prev, v_cur], axis=0), ones], axis=1)
                num = jnp.dot(e, v_ext, preferred_element_type=F32)
                if dil == 1:
                    dst = (bi, slice(start, start + BLOCK), slice(None))
                else:
                    dst = (bi, pl.ds(start, BLOCK, stride=dil), slice(None))
                num_scr[dst] = num[:, :HEAD_DIM]
                m_scr[dst] = jnp.broadcast_to(m, (BLOCK, LANES))
                s_scr[dst] = num[:, HEAD_DIM:]
                k_prev, v_prev = k_cur, v_cur
    m_all = jnp.maximum(jnp.maximum(m_scr[0], m_scr[1]), m_scr[2])
    top = jnp.zeros_like(m_all)
    bot = jnp.zeros_like(m_all)
    for bi in range(len(A_BRANCHES)):
        w = jnp.exp2(m_scr[bi] - m_all)
        top += w * num_scr[bi]
        bot += w * s_scr[bi]
    o_ref[...] = (top / bot).astype(o_ref.dtype)


def _attn_a_prompt(qkv, bias_a):
    t = qkv.shape[0]
    n_chunks = t // CHUNK_A
    blk = (CHUNK_A, HEAD_DIM)
    prev = lambda b: jnp.maximum(b - 1, 0)
    blocks = (2 * (5 * _nbytes(blk, F32) + _nbytes((3, BLOCK, 2 * BLOCK), F32) + _nbytes(blk, BF16))
              + 9 * _nbytes(blk, F32))
    return pl.pallas_call(
        _attn_a_prompt_kernel,
        grid=(n_chunks, N_HEADS_A),
        in_specs=[pl.BlockSpec(blk, lambda b, h: (b, COL_QA + h)),
                  pl.BlockSpec(blk, lambda b, h: (b, COL_KA + h)),
                  pl.BlockSpec(blk, lambda b, h: (prev(b), COL_KA + h)),
                  pl.BlockSpec(blk, lambda b, h: (b, COL_VA + h)),
                  pl.BlockSpec(blk, lambda b, h: (prev(b), COL_VA + h)),
                  pl.BlockSpec((3, None, BLOCK, 2 * BLOCK), lambda b, h: (0, h, 0, 0))],
        out_specs=pl.BlockSpec(blk, lambda b, h: (b, h)),
        out_shape=jax.ShapeDtypeStruct((t, QA_W), BF16),
        scratch_shapes=[pltpu.VMEM((3, CHUNK_A, HEAD_DIM), F32)] * 3,
        compiler_params=pltpu.CompilerParams(dimension_semantics=("parallel", "parallel"),
                                             vmem_limit_bytes=_vmem_limit(blocks)),
        name="mixer_a_prompt",
    )(qkv, qkv, qkv, qkv, qkv, bias_a)


def _attn_b_prompt_kernel(sink_ref, q_ref, kc_ref, kp_ref, vc_ref, vp_ref, bias_ref, o_ref):
    first_chunk = pl.program_id(0) == 0
    g = pl.program_id(1)
    col = lax.broadcasted_iota(jnp.int32, (BLOCK, 2 * BLOCK), 1)
    k_prev = kp_ref[...].astype(BF16)
    v_prev = vp_ref[...].astype(BF16)
    for blk in range(q_ref.shape[0] // BLOCK):
        rows = slice(blk * BLOCK, (blk + 1) * BLOCK)
        k_cur = kc_ref[rows, :].astype(BF16)
        v_cur = vc_ref[rows, :].astype(BF16)
        kcat = jnp.concatenate([k_prev, k_cur], axis=0)
        vcat = jnp.concatenate([v_prev, v_cur], axis=0)
        for j in range(GROUP_B):
            cols = slice(j * HEAD_DIM, (j + 1) * HEAD_DIM)
            sink = sink_ref[g * GROUP_B + j]
            z = _dot_nt((q_ref[rows, cols] * QK_SCALE_LOG2).astype(BF16), kcat) + bias_ref[j]
            if blk == 0:
                z = jnp.where(jnp.logical_and(first_chunk, col < BLOCK), NEG, z)
            m = jnp.maximum(jnp.max(z, axis=-1, keepdims=True), sink)
            e = jnp.exp2(z - m)
            denom = jnp.sum(e, axis=-1, keepdims=True) + jnp.exp2(sink - m)
            num = jnp.dot(e.astype(BF16), vcat, preferred_element_type=F32)
            o_ref[rows, cols] = (num / denom).astype(o_ref.dtype)
        k_prev, v_prev = k_cur, v_cur


def _attn_b_prompt(qkv, bias_b, sinks):
    t = qkv.shape[0]
    n_chunks = t // CHUNK_B
    per = CHUNK_B // BLOCK
    qblk = (CHUNK_B, GROUP_B * HEAD_DIM)
    kblk = (CHUNK_B, HEAD_DIM)
    pblk = (BLOCK, HEAD_DIM)
    qcol = COL_QB // GROUP_B
    prev = lambda b: jnp.maximum(b * per - 1, 0)
    blocks = 2 * (_nbytes(qblk, F32) + 2 * _nbytes(kblk, F32) + 2 * _nbytes(pblk, F32)
                  + _nbytes((GROUP_B, BLOCK, 2 * BLOCK), F32) + _nbytes(qblk, BF16))
    return pl.pallas_call(
        _attn_b_prompt_kernel,
        grid=(n_chunks, N_KV_B),
        in_specs=[pl.BlockSpec(memory_space=pltpu.SMEM),
                  pl.BlockSpec(qblk, lambda b, g: (b, qcol + g)),
                  pl.BlockSpec(kblk, lambda b, g: (b, COL_KB + g)),
                  pl.BlockSpec(pblk, lambda b, g: (prev(b), COL_KB + g)),
                  pl.BlockSpec(kblk, lambda b, g: (b, COL_VB + g)),
                  pl.BlockSpec(pblk, lambda b, g: (prev(b), COL_VB + g)),
                  pl.BlockSpec((None, GROUP_B, BLOCK, 2 * BLOCK), lambda b, g: (g, 0, 0, 0))],
        out_specs=pl.BlockSpec(qblk, lambda b, g: (b, g)),
        out_shape=jax.ShapeDtypeStruct((t, QB_W), BF16),
        compiler_params=pltpu.CompilerParams(dimension_semantics=("parallel", "parallel"),
                                             vmem_limit_bytes=_vmem_limit(blocks)),
        name="mixer_b_prompt",
    )(sinks.astype(F32), qkv, qkv, qkv, qkv, qkv, bias_b)


def _heads_to_rows(x, n_heads):
    return jnp.concatenate([x[:, h * HEAD_DIM:(h + 1) * HEAD_DIM] for h in range(n_heads)], axis=0)


def _rows_to_heads(x, n_heads):
    s = x.shape[0] // n_heads
    return jnp.concatenate([x[h * s:(h + 1) * s, :] for h in range(n_heads)], axis=1)


def _attn_sample_kernel(qkv_ref, kx_ref, ky_ref, vx_ref, vy_ref, kb_ref, vb_ref,
                        mba_ref, mbb_ref, sink_ref, o_ref,
                        ka_scr, va_scr, kb_scr, vb_scr):
    def new_rows(col0, n_heads, n_rows):
        new = _heads_to_rows(qkv_ref[:, col0 * LANES:(col0 + n_heads) * LANES], n_heads)
        return jnp.concatenate([new, jnp.zeros((n_rows - new.shape[0], HEAD_DIM), F32)], axis=0).astype(BF16)

    def fill_a(scr, x_ref, y_ref, col0):
        scr[0:SA_X_COLS, :] = x_ref[...].reshape(SA_X_COLS, HEAD_DIM).astype(BF16)
        scr[SA_X_COLS:SA_X_COLS + SA_Y_COLS, :] = y_ref[...].reshape(SA_Y_COLS, HEAD_DIM).astype(BF16)
        scr[SA_X_COLS + SA_Y_COLS:SA_COLS, :] = new_rows(col0, N_HEADS_A, SA_NEW_COLS)

    def fill_b(scr, c_ref, col0):
        scr[0:SB_CACHE_COLS, :] = c_ref[...].astype(BF16)
        scr[SB_CACHE_COLS:SB_COLS, :] = new_rows(col0, N_KV_B, SB_NEW_COLS)

    def queries(col0, n_heads):
        q = _heads_to_rows(qkv_ref[:, col0 * LANES:(col0 + n_heads) * LANES], n_heads)
        return (q * QK_SCALE_LOG2).astype(BF16)

    fill_a(ka_scr, kx_ref, ky_ref, COL_KA)
    fill_a(va_scr, vx_ref, vy_ref, COL_VA)
    fill_b(kb_scr, kb_ref, COL_KB)
    fill_b(vb_scr, vb_ref, COL_VB)

    z = _dot_nt(queries(COL_QA, N_HEADS_A), ka_scr[...]) + mba_ref[...]
    p = jnp.exp2(z - jnp.max(z, axis=-1, keepdims=True))
    denom = jnp.sum(p, axis=-1, keepdims=True)
    oa = jnp.dot(p.astype(BF16), va_scr[...], preferred_element_type=F32) / denom
    o_ref[:, 0:QA_W] = _rows_to_heads(oa, N_HEADS_A)

    zb = _dot_nt(queries(COL_QB, N_HEADS_B), kb_scr[...]) + mbb_ref[...]
    sink = sink_ref[:, 0:1]
    mb = jnp.maximum(jnp.max(zb, axis=-1, keepdims=True), sink)
    eb = jnp.exp2(zb - mb)
    denom_b = jnp.sum(eb, axis=-1, keepdims=True) + jnp.exp2(sink - mb)
    ob = jnp.dot(eb.astype(BF16), vb_scr[...], preferred_element_type=F32) / denom_b
    o_ref[:, QA_W:QA_W + QB_W] = _rows_to_heads(ob, N_HEADS_B)


def _attn_sample(qkv, layer, cache_ak, cache_av, cache_bk, cache_bv, mb_a, mb_b, sink_rows, s_new):
    depth, db, la = cache_ak.shape[0], cache_ak.shape[1], cache_ak.shape[2]
    lb = cache_bk.shape[2]
    ax_k = cache_ak.reshape(depth, db, la // DIL_MAX, DIL_MAX, N_HEADS_A, HEAD_DIM)
    ax_v = cache_av.reshape(depth, db, la // DIL_MAX, DIL_MAX, N_HEADS_A, HEAD_DIM)
    bk = cache_bk.reshape(depth, db, lb * N_KV_B, HEAD_DIM)
    bv = cache_bv.reshape(depth, db, lb * N_KV_B, HEAD_DIM)
    xblk = (None, None, SA_X_M, SUBLANES, N_HEADS_A, HEAD_DIM)
    yblk = (None, None, SA_Y_POS, N_HEADS_A, HEAD_DIM)
    bblk = (None, None, SB_CACHE_COLS, HEAD_DIM)
    y_idx = (la - SA_Y_POS) // SA_Y_POS
    rows_q = N_HEADS_A * s_new
    const = lambda b: (0, 0)
    blocks = (2 * (_nbytes((s_new, IN_W), F32) + 2 * _nbytes((SA_X_COLS + SA_Y_COLS, HEAD_DIM), F32)
                   + 2 * _nbytes((SB_CACHE_COLS, HEAD_DIM), F32)
                   + _nbytes((rows_q, SA_COLS), F32) + _nbytes((rows_q, SB_COLS), F32)
                   + _nbytes((rows_q, LANES), F32) + _nbytes((s_new, QA_W + QB_W), F32))
              + 2 * _nbytes((SA_COLS, HEAD_DIM), BF16) + 2 * _nbytes((SB_COLS, HEAD_DIM), BF16)
              + 4 * _nbytes((rows_q, SA_COLS), F32))
    return pl.pallas_call(
        _attn_sample_kernel,
        grid=(db,),
        in_specs=[pl.BlockSpec((s_new, IN_W), lambda b: (b, 0)),
                  pl.BlockSpec(xblk, lambda b: (layer, b, 0, 0, 0, 0)),
                  pl.BlockSpec(yblk, lambda b: (layer, b, y_idx, 0, 0)),
                  pl.BlockSpec(xblk, lambda b: (layer, b, 0, 0, 0, 0)),
                  pl.BlockSpec(yblk, lambda b: (layer, b, y_idx, 0, 0)),
                  pl.BlockSpec(bblk, lambda b: (layer, b, 0, 0)),
                  pl.BlockSpec(bblk, lambda b: (layer, b, 0, 0)),
                  pl.BlockSpec((rows_q, SA_COLS), const),
                  pl.BlockSpec((rows_q, SB_COLS), const),
                  pl.BlockSpec((rows_q, LANES), const)],
        out_specs=pl.BlockSpec((s_new, QA_W + QB_W), lambda b: (b, 0)),
        out_shape=jax.ShapeDtypeStruct((db * s_new, QA_W + QB_W), F32),
        scratch_shapes=[pltpu.VMEM((SA_COLS, HEAD_DIM), BF16), pltpu.VMEM((SA_COLS, HEAD_DIM), BF16),
                        pltpu.VMEM((SB_COLS, HEAD_DIM), BF16), pltpu.VMEM((SB_COLS, HEAD_DIM), BF16)],
        compiler_params=pltpu.CompilerParams(dimension_semantics=("parallel",),
                                             vmem_limit_bytes=_vmem_limit(blocks)),
        name="mixers_sample",
    )(qkv, ax_k, cache_ak, ax_v, cache_av, bk, bv, mb_a, mb_b, sink_rows)


def _outproj_kernel(x_ref, ma_ref, mb_ref, wa_ref, wb_ref, gt_ref, o_ref):
    acc = jnp.dot(ma_ref[...].astype(BF16), wa_ref[...], preferred_element_type=F32)
    acc += jnp.dot(mb_ref[...].astype(BF16), wb_ref[...], preferred_element_type=F32)
    o_ref[...] = x_ref[...] + gt_ref[...] * acc.reshape(o_ref.shape)


def _out_projection(x3, mix_a, mix_b, col_a, col_b, layer, w_out_bf, mod3, bb, rb):
    nb, r, d = x3.shape
    tm = bb * rb
    tn = 1024
    rblocks = r // rb
    n_m = (nb // bb) * rblocks
    kh = QA_W
    blocks = 2 * (2 * _nbytes((tm, tn), F32) + _nbytes((tm, kh), mix_a.dtype) + _nbytes((tm, kh), mix_b.dtype)
                  + 2 * _nbytes((kh, tn), BF16) + _nbytes((bb, SUBLANES, tn), F32))
    return pl.pallas_call(
        _outproj_kernel,
        grid=(n_m, d // tn),
        in_specs=[pl.BlockSpec((bb, rb, tn), lambda i, j: (i // rblocks, i % rblocks, j)),
                  pl.BlockSpec((tm, kh), lambda i, j: (i, col_a)),
                  pl.BlockSpec((tm, kh), lambda i, j: (i, col_b)),
                  pl.BlockSpec((None, kh, tn), lambda i, j: (layer, 0, j)),
                  pl.BlockSpec((None, kh, tn), lambda i, j: (layer, 1, j)),
                  pl.BlockSpec((bb, 1, tn), lambda i, j: (i // rblocks, 0, 2 * (d // tn) + j))],
        out_specs=pl.BlockSpec((bb, rb, tn), lambda i, j: (i // rblocks, i % rblocks, j)),
        out_shape=jax.ShapeDtypeStruct(x3.shape, F32),
        compiler_params=pltpu.CompilerParams(dimension_semantics=("parallel", "parallel"),
                                             vmem_limit_bytes=_vmem_limit(blocks)),
        name="out_projection",
    )(x3, mix_a, mix_b, w_out_bf, w_out_bf, mod3)


def _ffn_kernel(x_ref, sc_ref, sh_ref, gt_ref, g_ref, wg_ref, wu_ref, wd_ref, wc_ref, bc_ref, hist_ref,
                o_ref, cs_ref, h_scr, gate_scr, up_scr, act_scr, carry_scr, *, rblocks, chunk_seqs, chunk_rows):
    i = pl.program_id(0)
    f = pl.program_id(1)
    n_f = pl.num_programs(1)
    bb, rb, d = x_ref.shape
    tm, tf = bb * rb, wg_ref.shape[1]

    @pl.when(f == 0)
    def _():
        _norm_modulate_rows(x_ref, g_ref, sc_ref, sh_ref, h_scr)
        o_ref[...] = jnp.zeros(o_ref.shape, F32)

    if rblocks > 1:
        @pl.when(i % rblocks == 0)
        def _():
            carry_scr[f] = hist_ref[...]
        history = lambda s0, n: carry_scr[f]
    else:
        history = lambda s0, n: hist_ref[s0:s0 + n]

    h = h_scr[...]
    gate_scr[0:SUBLANES, :] = jnp.zeros((SUBLANES, tf), F32)
    gate_scr[SUBLANES:, :] = jnp.dot(h, wg_ref[...], preferred_element_type=F32)
    up_scr[...] = jnp.dot(h, wu_ref[...], preferred_element_type=F32)

    n_rows = chunk_seqs * chunk_rows
    shape = (chunk_seqs, chunk_rows, tf)
    t = lax.broadcasted_iota(jnp.int32, shape, 1)
    for r0 in range(0, tm, n_rows):
        def shifted(s):
            lo = SUBLANES + r0 - s
            return gate_scr[lo:lo + n_rows, :].reshape(shape)
        g0, g1, g2 = shifted(0), shifted(1), shifted(2)
        if r0 % rb == 0:
            hist = history(r0 // rb, chunk_seqs)
            g1 = jnp.where(t >= 1, g1, hist[:, 1:2, :])
            g2 = jnp.where(t >= 2, g2, jnp.where(t == 0, hist[:, 0:1, :], hist[:, 1:2, :]))
        gc = bc_ref[...] + wc_ref[0:1, :] * g2 + wc_ref[1:2, :] * g1 + wc_ref[2:3, :] * g0
        act = jax.nn.gelu(gc, approximate=True) * up_scr[r0:r0 + n_rows, :].reshape(shape)
        act_scr[r0:r0 + n_rows, :] = act.reshape(n_rows, tf).astype(BF16)
        if (r0 + n_rows) % rb == 0:
            s0 = (r0 + n_rows) // rb - chunk_seqs
            tail = g0[:, chunk_rows - (CONV_W - 1):, :]
            cs_ref[s0:s0 + chunk_seqs] = tail
            if rblocks > 1:
                carry_scr[f] = tail
    o_ref[...] += jnp.dot(act_scr[...], wd_ref[...], preferred_element_type=F32).reshape(bb, rb, d)

    @pl.when(f == n_f - 1)
    def _():
        o_ref[...] = x_ref[...] + gt_ref[...] * o_ref[...]


FFN_CHUNK_ROWS = 64


def _conv_ffn(x3, mod3, layer, g_ffn, wg_bf, wu_bf, wd_bf, w_conv, b_conv, hist, bb, rb):
    nb, r, d = x3.shape
    f_pad = wg_bf.shape[2]
    tf = FF_TILE
    n_f = f_pad // tf
    tm = bb * rb
    rblocks = r // rb
    assert rblocks == 1 or bb == 1
    n_m = (nb // bb) * rblocks
    carry_slots = n_f if rblocks > 1 else 1
    chunk_rows = min(rb, FFN_CHUNK_ROWS)
    chunk_seqs = FFN_CHUNK_ROWS // chunk_rows
    assert rb % chunk_rows == 0 and bb % chunk_seqs == 0 and chunk_rows >= CONV_W - 1
    modmap = lambda k: (lambda i, f: (i // rblocks, 0, k))
    blocks = (2 * (2 * _nbytes((tm, d), F32) + 2 * _nbytes((d, tf), BF16) + _nbytes((tf, d), BF16)
                   + 2 * _nbytes((bb, SUBLANES, tf), F32) + 3 * _nbytes((bb, SUBLANES, d), F32))
              + _nbytes((tm, d), BF16) + carry_slots * _nbytes((bb, SUBLANES, tf), F32)
              + 3 * _nbytes((tm + SUBLANES, tf), F32))
    y, tails = pl.pallas_call(
        functools.partial(_ffn_kernel, rblocks=rblocks, chunk_seqs=chunk_seqs, chunk_rows=chunk_rows),
        grid=(n_m, n_f),
        in_specs=[pl.BlockSpec((bb, rb, d), lambda i, f: (i // rblocks, i % rblocks, 0)),
                  pl.BlockSpec((bb, 1, d), modmap(4)),
                  pl.BlockSpec((bb, 1, d), modmap(3)),
                  pl.BlockSpec((bb, 1, d), modmap(5)),
                  pl.BlockSpec((1, d), lambda i, f: (0, 0)),
                  pl.BlockSpec((None, d, tf), lambda i, f: (layer, 0, f)),
                  pl.BlockSpec((None, d, tf), lambda i, f: (layer, 0, f)),
                  pl.BlockSpec((None, tf, d), lambda i, f: (layer, f, 0)),
                  pl.BlockSpec((None, CONV_W, tf), lambda i, f: (layer, 0, f)),
                  pl.BlockSpec((None, 1, tf), lambda i, f: (layer, 0, f)),
                  pl.BlockSpec((bb, CONV_W - 1, tf), lambda i, f: (i // rblocks, 0, f))],
        out_specs=[pl.BlockSpec((bb, rb, d), lambda i, f: (i // rblocks, i % rblocks, 0)),
                   pl.BlockSpec((bb, CONV_W - 1, tf), lambda i, f: (i, 0, f))],
        out_shape=[jax.ShapeDtypeStruct(x3.shape, F32),
                   jax.ShapeDtypeStruct((n_m * bb, CONV_W - 1, f_pad), F32)],
        scratch_shapes=[pltpu.VMEM((tm, d), BF16),
                        pltpu.VMEM((tm + SUBLANES, tf), F32),
                        pltpu.VMEM((tm, tf), F32),
                        pltpu.VMEM((tm, tf), BF16),
                        pltpu.VMEM((carry_slots, bb, CONV_W - 1, tf), F32)],
        compiler_params=pltpu.CompilerParams(dimension_semantics=("arbitrary", "arbitrary"),
                                             vmem_limit_bytes=_vmem_limit(blocks)),
        name="conv_ffn",
    )(x3, mod3, mod3, mod3, g_ffn.reshape(1, d), wg_bf, wu_bf, wd_bf, w_conv, b_conv, hist)
    state = tails.reshape(nb // bb, rblocks, bb, CONV_W - 1, f_pad)[:, rblocks - 1]
    return y, state.reshape(nb, CONV_W - 1, f_pad)


def _pad_last(a, n):
    return jnp.pad(a, [(0, 0)] * (a.ndim - 1) + [(0, n - a.shape[-1])])


def kernel(x_prompt, x_sample, c_prompt, c_sample, cache_a_k, cache_a_v, cache_b_k, cache_b_v, state_conv, rel_bias, w_ada, b_ada, g_attn, g_ffn, w_in, g_qn_a, g_kn_a, g_qn_b, g_kn_b, sinks, w_out, w_gate, w_up, w_conv, b_conv, w_down):
    depth = w_in.shape[0]
    batch, seq, d = x_prompt.shape
    db, s_new, _ = x_sample.shape
    d_ff = w_gate.shape[2]
    f_pad = -(-d_ff // FF_TILE) * FF_TILE
    la, lb = cache_a_k.shape[2], cache_b_k.shape[2]
    assert batch == 1 and seq % CHUNK_A == 0 and s_new == SUBLANES

    pb = _bias_by_distance(rel_bias, max(WIN_A + 1, la + s_new)) * LOG2E
    bias_a, bias_b = _prompt_bias_tables(pb)
    mb_a, mb_b = _sample_tables(pb, s_new, la, lb)
    sinks2 = sinks.astype(F32) * LOG2E

    c_all = jnp.concatenate([c_prompt, jnp.zeros((SUBLANES - batch, d), F32), c_sample], axis=0)

    ones = jnp.ones((HEAD_DIM,), F32)
    flag = jnp.concatenate([jnp.ones((2 * QA_W,), F32), jnp.zeros((QA_W,), F32), jnp.ones((QB_W + KB_W,), F32),
                            jnp.zeros((KB_W,), F32)]).reshape(1, IN_W)

    w_in_bf = w_in.astype(BF16)
    w_out_bf = w_out.astype(BF16)
    wg_bf = _pad_last(w_gate.astype(BF16), f_pad)
    wu_bf = _pad_last(w_up.astype(BF16), f_pad)
    wd_bf = jnp.pad(w_down.astype(BF16), ((0, 0), (0, f_pad - d_ff), (0, 0)))
    wc = _pad_last(w_conv, f_pad)
    bc = _pad_last(b_conv, f_pad).reshape(depth, 1, f_pad)
    mod_all = _modulation(c_all, w_ada, b_ada)

    xp = x_prompt
    xs = x_sample
    hist_p = jnp.zeros((batch, CONV_W - 1, f_pad), F32)
    outs = [[] for _ in range(10)]
    for l in range(depth):
        gain = jnp.concatenate([jnp.tile(g_qn_a[l], N_HEADS_A), jnp.tile(g_kn_a[l], N_HEADS_A),
                                jnp.tile(ones, N_HEADS_A), jnp.tile(g_qn_b[l], N_HEADS_B),
                                jnp.tile(g_kn_b[l], N_KV_B), jnp.tile(ones, N_KV_B)]).reshape(1, IN_W)
        sink_rows = jnp.broadcast_to(jnp.repeat(sinks2[l], s_new)[:, None], (N_HEADS_B * s_new, LANES))
        mod_p = mod_all[l, 0:batch].reshape(batch, 1, 6 * d)
        mod_s = mod_all[l, SUBLANES:].reshape(db, 1, 6 * d)

        qkv_p = _projection(xp, mod_p, l, g_attn[l], w_in_bf, gain, flag, 1, 1024)
        mix_pa = _attn_a_prompt(qkv_p, bias_a)
        mix_pb = _attn_b_prompt(qkv_p, bias_b, sinks2[l])
        xp = _out_projection(xp, mix_pa, mix_pb, 0, 0, l, w_out_bf, mod_p, 1, 1024)
        xp, conv_p = _conv_ffn(xp, mod_p, l, g_ffn[l], wg_bf, wu_bf, wd_bf, wc, bc, hist_p, 1, 512)
        keep_a, keep_b = min(WIN_A, seq), min(WIN_B, seq)
        outs[0].append(qkv_p[seq - keep_a:, COL_KA * LANES:COL_KA * LANES + QA_W].reshape(batch, keep_a, N_HEADS_A, HEAD_DIM))
        outs[1].append(qkv_p[seq - keep_a:, COL_VA * LANES:COL_VA * LANES + QA_W].reshape(batch, keep_a, N_HEADS_A, HEAD_DIM))
        outs[2].append(qkv_p[seq - keep_b:, COL_KB * LANES:COL_KB * LANES + KB_W].reshape(batch, keep_b, N_KV_B, HEAD_DIM))
        outs[3].append(qkv_p[seq - keep_b:, COL_VB * LANES:COL_VB * LANES + KB_W].reshape(batch, keep_b, N_KV_B, HEAD_DIM))
        outs[4].append(conv_p[:, :, :d_ff])

        qkv_s = _projection(xs, mod_s, l, g_attn[l], w_in_bf, gain, flag, db, s_new)
        mix_s = _attn_sample(qkv_s, l, cache_a_k, cache_a_v, cache_b_k, cache_b_v,
                             mb_a, mb_b, sink_rows, s_new)
        xs = _out_projection(xs, mix_s, mix_s, 0, 1, l, w_out_bf, mod_s, db, s_new)
        xs, conv_s = _conv_ffn(xs, mod_s, l, g_ffn[l], wg_bf, wu_bf, wd_bf, wc, bc,
                               _pad_last(state_conv[l], f_pad), db // 2, s_new)
        outs[5].append(qkv_s[:, COL_KA * LANES:COL_KA * LANES + QA_W].reshape(db, s_new, N_HEADS_A, HEAD_DIM))
        outs[6].append(qkv_s[:, COL_VA * LANES:COL_VA * LANES + QA_W].reshape(db, s_new, N_HEADS_A, HEAD_DIM))
        outs[7].append(qkv_s[:, COL_KB * LANES:COL_KB * LANES + KB_W].reshape(db, s_new, N_KV_B, HEAD_DIM))
        outs[8].append(qkv_s[:, COL_VB * LANES:COL_VB * LANES + KB_W].reshape(db, s_new, N_KV_B, HEAD_DIM))
        outs[9].append(conv_s[:, :, :d_ff])

    return (xp, xs) + tuple(jnp.stack(o) for o in outs)
```

```python
import functools
import math

import numpy as np
import jax
import jax.numpy as jnp
from jax import lax
from jax.experimental import pallas as pl
from jax.experimental.pallas import tpu as pltpu

F32 = jnp.float32
BF16 = jnp.bfloat16

LANES = 128
SUBLANES = 8
VMEM_BYTES_V7X = 64 * 1024 * 1024

HEAD_DIM = 128
N_HEADS_A = 8
N_HEADS_B = 8
N_KV_B = 2
GROUP_B = N_HEADS_B // N_KV_B
A_BRANCHES = ((128, 1), (512, 4), (2048, 16))
WIN_A = 2048
WIN_B = 128
BLOCK = 128
N_BUCKETS = 32
MAX_DISTANCE = 2048
CONV_W = 3
EPS = 1e-6
NEG = -1e30
SCALE = HEAD_DIM ** -0.5
LOG2E = math.log2(math.e)
QK_SCALE_LOG2 = SCALE * LOG2E

QA_W = N_HEADS_A * HEAD_DIM
QB_W = N_HEADS_B * HEAD_DIM
KB_W = N_KV_B * HEAD_DIM
IN_W = 3 * QA_W + QB_W + 2 * KB_W
COL_QA, COL_KA, COL_VA = 0, QA_W // LANES, 2 * QA_W // LANES
COL_QB = 3 * QA_W // LANES
COL_KB = COL_QB + QB_W // LANES
COL_VB = COL_KB + KB_W // LANES

CHUNK_A = WIN_A
CHUNK_B = 1024
FF_TILE = 512


def _vmem_limit(block_bytes):
    return int(min(VMEM_BYTES_V7X - (4 << 20), block_bytes + (12 << 20)))


def _nbytes(shape, dtype):
    return int(np.prod(shape)) * jnp.dtype(dtype).itemsize


def _t5_bucket_np(dist):
    dist = np.maximum(dist, 0)
    max_exact = N_BUCKETS // 2
    ratio = np.log(np.maximum(dist, 1).astype(np.float64) / max_exact) / math.log(MAX_DISTANCE / max_exact)
    large = max_exact + (ratio * (N_BUCKETS - max_exact)).astype(np.int32)
    large = np.minimum(large, N_BUCKETS - 1)
    return np.where(dist < max_exact, dist, large).astype(np.int32)


def _bias_by_distance(rel_bias, n):
    buckets = _t5_bucket_np(np.arange(n))
    assert np.all(np.diff(buckets) >= 0)
    runs = np.bincount(buckets, minlength=N_BUCKETS)
    tab = rel_bias.astype(F32).T
    return jnp.concatenate([jnp.broadcast_to(tab[:, k:k + 1], (tab.shape[0], int(runs[k])))
                            for k in range(N_BUCKETS) if runs[k] > 0], axis=1)


def _banded_table(vals):
    hh, n = vals.shape
    period = 4 * BLOCK
    w = jnp.concatenate([jnp.full((hh, BLOCK - 1), NEG, F32), vals,
                         jnp.full((hh, period - (BLOCK - 1) - n), NEG, F32)], axis=1)
    skew = jnp.tile(w, (1, BLOCK + 1))[:, :BLOCK * (period + 1)].reshape(hh, BLOCK, period + 1)
    return jnp.flip(skew[:, :, :2 * BLOCK], axis=2)


def _prompt_bias_tables(pb):
    bias_a = jnp.stack([_banded_table(pb[:N_HEADS_A, 0:BLOCK * dil + 1:dil]) for _, dil in A_BRANCHES])
    bias_b = _banded_table(pb[N_HEADS_A:, 0:WIN_B])
    return bias_a, bias_b.reshape(N_KV_B, GROUP_B, BLOCK, 2 * BLOCK)


DIL_MAX = A_BRANCHES[-1][1]
SA_Y_POS = A_BRANCHES[1][0]
SA_X_M = (WIN_A - SA_Y_POS) // DIL_MAX
SA_X_COLS = SA_X_M * SUBLANES * N_HEADS_A
SA_Y_COLS = SA_Y_POS * N_HEADS_A
SA_CHUNK = 768
SA_NEW_COLS = -(SA_X_COLS + SA_Y_COLS + N_HEADS_A * SUBLANES) % SA_CHUNK + N_HEADS_A * SUBLANES
SA_COLS = SA_X_COLS + SA_Y_COLS + SA_NEW_COLS
assert SA_COLS % SA_CHUNK == 0 and SA_NEW_COLS % (2 * SUBLANES) == 0
SB_CACHE_COLS = WIN_B * N_KV_B
SB_NEW_COLS = LANES
SB_COLS = SB_CACHE_COLS + SB_NEW_COLS


def _sample_columns(s_new, la, lb):
    m, res, h = np.meshgrid(np.arange(SA_X_M), np.arange(SUBLANES), np.arange(N_HEADS_A), indexing="ij")
    pos_x, head_x = (DIL_MAX * m + res).reshape(-1), h.reshape(-1)
    p, h = np.meshgrid(np.arange(la - SA_Y_POS, la), np.arange(N_HEADS_A), indexing="ij")
    pos_y, head_y = p.reshape(-1), h.reshape(-1)
    h, t = np.meshgrid(np.arange(N_HEADS_A), np.arange(s_new), indexing="ij")
    pad = np.full((SA_NEW_COLS - N_HEADS_A * s_new,), -1)
    pos_a = np.concatenate([pos_x, pos_y, la + t.reshape(-1), pad])
    head_a = np.concatenate([head_x, head_y, h.reshape(-1), pad])
    p, g = np.meshgrid(np.arange(lb), np.arange(N_KV_B), indexing="ij")
    g2, t = np.meshgrid(np.arange(N_KV_B), np.arange(s_new), indexing="ij")
    pad = np.full((SB_NEW_COLS - N_KV_B * s_new,), -1)
    pos_b = np.concatenate([p.reshape(-1), lb + t.reshape(-1), pad])
    head_b = np.concatenate([g.reshape(-1), g2.reshape(-1), pad])
    return pos_a, head_a, pos_b, head_b


def _query_key_bias(pbh, length, s_new):
    n = length + s_new
    rev = jnp.pad(jnp.flip(pbh[:, :n], axis=1), ((0, 0), (0, s_new - 1)))
    return jnp.stack([rev[:, s_new - 1 - i:s_new - 1 - i + n] for i in range(s_new)], axis=1)


def _per_key_head(a, n_key_heads, key_head_of, head_major):
    hh = a.shape[0]
    own = jnp.asarray(np.array([[key_head_of(h) == g for g in range(n_key_heads)] for h in range(hh)]))
    if head_major:
        out = jnp.where(own[:, None, :, None], a[:, :, None, :], NEG)
    else:
        out = jnp.where(own[:, None, None, :], a[:, :, :, None], NEG)
    return out.reshape(hh * a.shape[1], -1)


def _sample_tables(pb, s_new, la, lb):
    assert la == WIN_A and lb == WIN_B and s_new == SUBLANES
    pos_a, head_a, pos_b, head_b = _sample_columns(s_new, la, lb)
    qh, qi = np.meshgrid(np.arange(N_HEADS_A), np.arange(s_new), indexing="ij")
    qh, qi = qh.reshape(-1, 1), qi.reshape(-1, 1)

    dist = la + qi - pos_a[None, :]
    real = (pos_a >= 0)[None, :] & (dist >= 0) & (head_a[None, :] == qh)
    count = np.zeros(dist.shape, np.float32)
    for win, dil in A_BRANCHES:
        count += (real & (dist % dil == 0) & (dist <= win)).astype(np.float32)
    a = _query_key_bias(pb[:N_HEADS_A], la, s_new)
    same = lambda h: h
    a_x = a[:, :, :la - SA_Y_POS].reshape(N_HEADS_A, s_new, SA_X_M, DIL_MAX)[:, :, :, :SUBLANES]
    mb_a = jnp.concatenate([
        _per_key_head(a_x.reshape(N_HEADS_A, s_new, -1), N_HEADS_A, same, False),
        _per_key_head(a[:, :, la - SA_Y_POS:la], N_HEADS_A, same, False),
        _per_key_head(a[:, :, la:], N_HEADS_A, same, True),
        jnp.full((N_HEADS_A * s_new, SA_NEW_COLS - N_HEADS_A * s_new), NEG, F32)], axis=1)
    mb_a = jnp.where(jnp.asarray(count > 0), mb_a + jnp.asarray(np.log2(np.maximum(count, 1.0))), NEG)

    dist_b = lb + qi - pos_b[None, :]
    valid_b = (pos_b >= 0)[None, :] & (dist_b >= 0) & (dist_b < WIN_B) & (head_b[None, :] == qh // GROUP_B)
    b = _query_key_bias(pb[N_HEADS_A:], lb, s_new)
    group = lambda h: h // GROUP_B
    mb_b = jnp.concatenate([
        _per_key_head(b[:, :, :lb], N_KV_B, group, False),
        _per_key_head(b[:, :, lb:], N_KV_B, group, True),
        jnp.full((N_HEADS_B * s_new, SB_NEW_COLS - N_KV_B * s_new), NEG, F32)], axis=1)
    mb_b = jnp.where(jnp.asarray(valid_b), mb_b, NEG)
    return mb_a, mb_b


def _split_bf16(a):
    hi = a.astype(BF16)
    lo = (a - hi.astype(F32)).astype(BF16)
    return hi, lo


def _mod_kernel(c_ref, w_ref, b_ref, o_ref):
    c = c_ref[...]
    s_hi, s_lo = _split_bf16(c * jax.nn.sigmoid(c))
    w_hi, w_lo = _split_bf16(w_ref[...])
    acc = jnp.dot(s_hi, w_hi, preferred_element_type=F32)
    acc += jnp.dot(s_hi, w_lo, preferred_element_type=F32)
    acc += jnp.dot(s_lo, w_hi, preferred_element_type=F32)
    o_ref[...] = acc + b_ref[...]


def _modulation(c_all, w_ada, b_ada):
    m, d = c_all.shape
    depth, _, n = w_ada.shape
    tn = 1024
    blocks = 2 * (_nbytes((d, tn), F32) + _nbytes((m, tn), F32)) + _nbytes((m, d), F32) * 2 + 2 * _nbytes((d, tn), BF16)
    return pl.pallas_call(
        _mod_kernel,
        grid=(depth, n // tn),
        in_specs=[pl.BlockSpec((m, d), lambda l, j: (0, 0)),
                  pl.BlockSpec((None, d, tn), lambda l, j: (l, 0, j)),
                  pl.BlockSpec((None, 1, tn), lambda l, j: (l, 0, j))],
        out_specs=pl.BlockSpec((None, m, tn), lambda l, j: (l, 0, j)),
        out_shape=jax.ShapeDtypeStruct((depth, m, n), F32),
        compiler_params=pltpu.CompilerParams(dimension_semantics=("parallel", "parallel"),
                                             vmem_limit_bytes=_vmem_limit(blocks)),
        name="modulation",
    )(c_all, w_ada, b_ada.reshape(depth, 1, n))


def _norm_modulate(x, g, sc, sh):
    ms = jnp.mean(x * x, axis=-1, keepdims=True)
    return (x * lax.rsqrt(ms + EPS) * g) * (1.0 + sc) + sh


NORM_ROWS = 16


def _norm_modulate_rows(x_ref, g_ref, sc_ref, sh_ref, h_scr):
    bb, rb, d = x_ref.shape
    cr = min(rb, NORM_ROWS)
    cs = NORM_ROWS // cr
    for r0 in range(0, bb * rb, NORM_ROWS):
        s, t0 = r0 // rb, r0 % rb
        h = _norm_modulate(x_ref[s:s + cs, t0:t0 + cr, :], g_ref[...], sc_ref[s:s + cs], sh_ref[s:s + cs])
        h_scr[r0:r0 + NORM_ROWS, :] = h.reshape(NORM_ROWS, d).astype(BF16)


def _dot_nt(a, b):
    return lax.dot_general(a, b, (((1,), (1,)), ((), ())), preferred_element_type=F32)


def _proj_kernel(x_ref, sc_ref, sh_ref, g_ref, w_ref, gain_ref, flag_ref, o_ref, h_scr, acc_scr):
    @pl.when(pl.program_id(1) == 0)
    def _():
        _norm_modulate_rows(x_ref, g_ref, sc_ref, sh_ref, h_scr)

    acc_scr[...] = jnp.dot(h_scr[...], w_ref[...], preferred_element_type=F32)
    tm, tn = acc_scr.shape
    for r0 in range(0, tm, 64):
        for c in range(tn // LANES):
            sl = slice(c * LANES, (c + 1) * LANES)
            blk = acc_scr[r0:r0 + 64, sl]
            ms = jnp.mean(blk * blk, axis=-1, keepdims=True)
            nrm = blk * lax.rsqrt(ms + EPS) * gain_ref[:, sl]
            o_ref[r0:r0 + 64, sl] = jnp.where(flag_ref[:, sl] > 0.0, nrm, blk)


def _projection(x3, mod3, layer, g_attn, w_in_bf, gain, flag, bb, rb):
    nb, r, d = x3.shape
    tm = bb * rb
    tn = 768
    n_m = (nb // bb) * (r // rb)
    rblocks = r // rb
    xmap = lambda i, j: (i // rblocks, i % rblocks, 0)
    blocks = (2 * (_nbytes((tm, d), F32) + _nbytes((d, tn), BF16) + _nbytes((tm, tn), F32))
              + _nbytes((tm, d), BF16) + 8 * _nbytes((bb, SUBLANES, d), F32) + _nbytes((tm, tn), F32))
    return pl.pallas_call(
        _proj_kernel,
        grid=(n_m, IN_W // tn),
        in_specs=[pl.BlockSpec((bb, rb, d), xmap),
                  pl.BlockSpec((bb, 1, d), lambda i, j: (i // rblocks, 0, 1)),
                  pl.BlockSpec((bb, 1, d), lambda i, j: (i // rblocks, 0, 0)),
                  pl.BlockSpec((1, d), lambda i, j: (0, 0)),
                  pl.BlockSpec((None, d, tn), lambda i, j: (layer, 0, j)),
                  pl.BlockSpec((1, tn), lambda i, j: (0, j)),
                  pl.BlockSpec((1, tn), lambda i, j: (0, j))],
        out_specs=pl.BlockSpec((tm, tn), lambda i, j: (i, j)),
        out_shape=jax.ShapeDtypeStruct((nb * r, IN_W), F32),
        scratch_shapes=[pltpu.VMEM((tm, d), BF16), pltpu.VMEM((tm, tn), F32)],
        compiler_params=pltpu.CompilerParams(dimension_semantics=("parallel", "arbitrary"),
                                             vmem_limit_bytes=_vmem_limit(blocks)),
        name="projection",
    )(x3, mod3, mod3, g_attn.reshape(1, d), w_in_bf, gain, flag)


def _rows(ref, start, dil):
    if dil == 1:
        return ref[start:start + BLOCK, :]
    return ref[pl.ds(start, BLOCK, stride=dil), :]


def _attn_a_prompt_kernel(q_ref, kc_ref, kp_ref, vc_ref, vp_ref, bias_ref, o_ref, num_scr, m_scr, s_scr):
    first_chunk = pl.program_id(0) == 0
    chunk = q_ref.shape[0]
    col = lax.broadcasted_iota(jnp.int32, (BLOCK, 2 * BLOCK), 1)
    ones = jnp.ones((2 * BLOCK, LANES), BF16)
    for bi, (_, dil) in enumerate(A_BRANCHES):
        nsub = chunk // (BLOCK * dil)
        bias = bias_ref[bi]
        for r in range(dil):
            last = r + dil * BLOCK * (nsub - 1)
            k_prev = _rows(kp_ref, last, dil).astype(BF16)
            v_prev = _rows(vp_ref, last, dil).astype(BF16)
            for ub in range(nsub):
                start = r + dil * BLOCK * ub
                q = (_rows(q_ref, start, dil) * QK_SCALE_LOG2).astype(BF16)
                k_cur = _rows(kc_ref, start, dil).astype(BF16)
                v_cur = _rows(vc_ref, start, dil).astype(BF16)
                z = _dot_nt(q, jnp.concatenate([k_prev, k_cur], axis=0)) + bias
                if ub == 0:
                    z = jnp.where(jnp.logical_and(first_chunk, col < BLOCK), NEG, z)
                m = jnp.max(z, axis=-1, keepdims=True)
                e = jnp.exp2(z - m).astype(BF16)
                v_ext = jnp.concatenate([jnp.concatenate([v_prev, v_cur], axis=0), ones], axis=1)
                num = jnp.dot(e, v_ext, preferred_element_type=F32)
                if dil == 1:
                    dst = (bi, slice(start, start + BLOCK), slice(None))
                else:
                    dst = (bi, pl.ds(start, BLOCK, stride=dil), slice(None))
                num_scr[dst] = num[:, :HEAD_DIM]
                m_scr[dst] = jnp.broadcast_to(m, (BLOCK, LANES))
                s_scr[dst] = num[:, HEAD_DIM:]
                k_prev, v_prev = k_cur, v_cur
    m_all = jnp.maximum(jnp.maximum(m_scr[0], m_scr[1]), m_scr[2])
    top = jnp.zeros_like(m_all)
    bot = jnp.zeros_like(m_all)
    for bi in range(len(A_BRANCHES)):
        w = jnp.exp2(m_scr[bi] - m_all)
        top += w * num_scr[bi]
        bot += w * s_scr[bi]
    o_ref[...] = (top / bot).astype(o_ref.dtype)


def _attn_a_prompt(qkv, bias_a):
    t = qkv.shape[0]
    n_chunks = t // CHUNK_A
    blk = (CHUNK_A, HEAD_DIM)
    prev = lambda b: jnp.maximum(b - 1, 0)
    blocks = (2 * (5 * _nbytes(blk, F32) + _nbytes((3, BLOCK, 2 * BLOCK), F32) + _nbytes(blk, BF16))
              + 9 * _nbytes(blk, F32))
    return pl.pallas_call(
        _attn_a_prompt_kernel,
        grid=(n_chunks, N_HEADS_A),
        in_specs=[pl.BlockSpec(blk, lambda b, h: (b, COL_QA + h)),
                  pl.BlockSpec(blk, lambda b, h: (b, COL_KA + h)),
                  pl.BlockSpec(blk, lambda b, h: (prev(b), COL_KA + h)),
                  pl.BlockSpec(blk, lambda b, h: (b, COL_VA + h)),
                  pl.BlockSpec(blk, lambda b, h: (prev(b), COL_VA + h)),
                  pl.BlockSpec((3, None, BLOCK, 2 * BLOCK), lambda b, h: (0, h, 0, 0))],
        out_specs=pl.BlockSpec(blk, lambda b, h: (b, h)),
        out_shape=jax.ShapeDtypeStruct((t, QA_W), BF16),
        scratch_shapes=[pltpu.VMEM((3, CHUNK_A, HEAD_DIM), F32)] * 3,
        compiler_params=pltpu.CompilerParams(dimension_semantics=("parallel", "parallel"),
                                             vmem_limit_bytes=_vmem_limit(blocks)),
        name="mixer_a_prompt",
    )(qkv, qkv, qkv, qkv, qkv, bias_a)


def _attn_b_prompt_kernel(sink_ref, q_ref, kc_ref, kp_ref, vc_ref, vp_ref, bias_ref, o_ref):
    first_chunk = pl.program_id(0) == 0
    g = pl.program_id(1)
    col = lax.broadcasted_iota(jnp.int32, (BLOCK, 2 * BLOCK), 1)
    k_prev = kp_ref[...].astype(BF16)
    v_prev = vp_ref[...].astype(BF16)
    for blk in range(q_ref.shape[0] // BLOCK):
        rows = slice(blk * BLOCK, (blk + 1) * BLOCK)
        k_cur = kc_ref[rows, :].astype(BF16)
        v_cur = vc_ref[rows, :].astype(BF16)
        kcat = jnp.concatenate([k_prev, k_cur], axis=0)
        vcat = jnp.concatenate([v_prev, v_cur], axis=0)
        for j in range(GROUP_B):
            cols = slice(j * HEAD_DIM, (j + 1) * HEAD_DIM)
            sink = sink_ref[g * GROUP_B + j]
            z = _dot_nt((q_ref[rows, cols] * QK_SCALE_LOG2).astype(BF16), kcat) + bias_ref[j]
            if blk == 0:
                z = jnp.where(jnp.logical_and(first_chunk, col < BLOCK), NEG, z)
            m = jnp.maximum(jnp.max(z, axis=-1, keepdims=True), sink)
            e = jnp.exp2(z - m)
            denom = jnp.sum(e, axis=-1, keepdims=True) + jnp.exp2(sink - m)
            num = jnp.dot(e.astype(BF16), vcat, preferred_element_type=F32)
            o_ref[rows, cols] = (num / denom).astype(o_ref.dtype)
        k_prev, v_prev = k_cur, v_cur


def _attn_b_prompt(qkv, bias_b, sinks):
    t = qkv.shape[0]
    n_chunks = t // CHUNK_B
    per = CHUNK_B // BLOCK
    qblk = (CHUNK_B, GROUP_B * HEAD_DIM)
    kblk = (CHUNK_B, HEAD_DIM)
    pblk = (BLOCK, HEAD_DIM)
    qcol = COL_QB // GROUP_B
    prev = lambda b: jnp.maximum(b * per - 1, 0)
    blocks = 2 * (_nbytes(qblk, F32) + 2 * _nbytes(kblk, F32) + 2 * _nbytes(pblk, F32)
                  + _nbytes((GROUP_B, BLOCK, 2 * BLOCK), F32) + _nbytes(qblk, BF16))
    return pl.pallas_call(
        _attn_b_prompt_kernel,
        grid=(n_chunks, N_KV_B),
        in_specs=[pl.BlockSpec(memory_space=pltpu.SMEM),
                  pl.BlockSpec(qblk, lambda b, g: (b, qcol + g)),
                  pl.BlockSpec(kblk, lambda b, g: (b, COL_KB + g)),
                  pl.BlockSpec(pblk, lambda b, g: (prev(b), COL_KB + g)),
                  pl.BlockSpec(kblk, lambda b, g: (b, COL_VB + g)),
                  pl.BlockSpec(pblk, lambda b, g: (prev(b), COL_VB + g)),
                  pl.BlockSpec((None, GROUP_B, BLOCK, 2 * BLOCK), lambda b, g: (g, 0, 0, 0))],
        out_specs=pl.BlockSpec(qblk, lambda b, g: (b, g)),
        out_shape=jax.ShapeDtypeStruct((t, QB_W), BF16),
        compiler_params=pltpu.CompilerParams(dimension_semantics=("parallel", "parallel"),
                                             vmem_limit_bytes=_vmem_limit(blocks)),
        name="mixer_b_prompt",
    )(sinks.astype(F32), qkv, qkv, qkv, qkv, qkv, bias_b)


def _heads_to_rows(x, n_heads):
    return jnp.concatenate([x[:, h * HEAD_DIM:(h + 1) * HEAD_DIM] for h in range(n_heads)], axis=0)


def _rows_to_heads(x, n_heads):
    s = x.shape[0] // n_heads
    return jnp.concatenate([x[h * s:(h + 1) * s, :] for h in range(n_heads)], axis=1)


def _attn_sample_kernel(qkv_ref, kx_ref, ky_ref, vx_ref, vy_ref, kb_ref, vb_ref,
                        mba_ref, mbb_ref, sink_ref, o_ref,
                        ka_scr, va_scr, kb_scr, vb_scr, z_scr):
    def new_rows(col0, n_heads, n_rows):
        new = _heads_to_rows(qkv_ref[:, col0 * LANES:(col0 + n_heads) * LANES], n_heads)
        return jnp.concatenate([new, jnp.zeros((n_rows - new.shape[0], HEAD_DIM), F32)], axis=0).astype(BF16)

    def fill_a(scr, x_ref, y_ref, col0):
        scr[0:SA_X_COLS, :] = x_ref[...].reshape(SA_X_COLS, HEAD_DIM).astype(BF16)
        scr[SA_X_COLS:SA_X_COLS + SA_Y_COLS, :] = y_ref[...].reshape(SA_Y_COLS, HEAD_DIM).astype(BF16)
        scr[SA_X_COLS + SA_Y_COLS:SA_COLS, :] = new_rows(col0, N_HEADS_A, SA_NEW_COLS)

    def fill_b(scr, c_ref, col0):
        scr[0:SB_CACHE_COLS, :] = c_ref[...].astype(BF16)
        scr[SB_CACHE_COLS:SB_COLS, :] = new_rows(col0, N_KV_B, SB_NEW_COLS)

    def queries(col0, n_heads):
        q = _heads_to_rows(qkv_ref[:, col0 * LANES:(col0 + n_heads) * LANES], n_heads)
        return (q * QK_SCALE_LOG2).astype(BF16)

    fill_a(ka_scr, kx_ref, ky_ref, COL_KA)
    fill_a(va_scr, vx_ref, vy_ref, COL_VA)
    fill_b(kb_scr, kb_ref, COL_KB)
    fill_b(vb_scr, vb_ref, COL_VB)

    qa = queries(COL_QA, N_HEADS_A)
    m = None
    for c0 in range(0, SA_COLS, SA_CHUNK):
        z = _dot_nt(qa, ka_scr[c0:c0 + SA_CHUNK, :]) + mba_ref[:, c0:c0 + SA_CHUNK]
        z_scr[:, c0:c0 + SA_CHUNK] = z
        zmax = jnp.max(z, axis=-1, keepdims=True)
        m = zmax if m is None else jnp.maximum(m, zmax)
    denom, oa = 0.0, 0.0
    for c0 in range(0, SA_COLS, SA_CHUNK):
        p = jnp.exp2(z_scr[:, c0:c0 + SA_CHUNK] - m)
        denom = denom + jnp.sum(p, axis=-1, keepdims=True)
        oa = oa + jnp.dot(p.astype(BF16), va_scr[c0:c0 + SA_CHUNK, :], preferred_element_type=F32)
    o_ref[:, 0:QA_W] = _rows_to_heads(oa / denom, N_HEADS_A)

    zb = _dot_nt(queries(COL_QB, N_HEADS_B), kb_scr[...]) + mbb_ref[...]
    sink = sink_ref[:, 0:1]
    mb = jnp.maximum(jnp.max(zb, axis=-1, keepdims=True), sink)
    eb = jnp.exp2(zb - mb)
    denom_b = jnp.sum(eb, axis=-1, keepdims=True) + jnp.exp2(sink - mb)
    ob = jnp.dot(eb.astype(BF16), vb_scr[...], preferred_element_type=F32) / denom_b
    o_ref[:, QA_W:QA_W + QB_W] = _rows_to_heads(ob, N_HEADS_B)


def _attn_sample(qkv, layer, cache_ak, cache_av, cache_bk, cache_bv, mb_a, mb_b, sink_rows, s_new):
    depth, db, la = cache_ak.shape[0], cache_ak.shape[1], cache_ak.shape[2]
    lb = cache_bk.shape[2]
    ax_k = cache_ak.reshape(depth, db, la // DIL_MAX, DIL_MAX, N_HEADS_A, HEAD_DIM)
    ax_v = cache_av.reshape(depth, db, la // DIL_MAX, DIL_MAX, N_HEADS_A, HEAD_DIM)
    bk = cache_bk.reshape(depth, db, lb * N_KV_B, HEAD_DIM)
    bv = cache_bv.reshape(depth, db, lb * N_KV_B, HEAD_DIM)
    xblk = (None, None, SA_X_M, SUBLANES, N_HEADS_A, HEAD_DIM)
    yblk = (None, None, SA_Y_POS, N_HEADS_A, HEAD_DIM)
    bblk = (None, None, SB_CACHE_COLS, HEAD_DIM)
    y_idx = (la - SA_Y_POS) // SA_Y_POS
    rows_q = N_HEADS_A * s_new
    const = lambda b: (0, 0)
    blocks = (2 * (_nbytes((s_new, IN_W), F32) + 2 * _nbytes((SA_X_COLS + SA_Y_COLS, HEAD_DIM), F32)
                   + 2 * _nbytes((SB_CACHE_COLS, HEAD_DIM), F32)
                   + _nbytes((rows_q, SA_COLS), F32) + _nbytes((rows_q, SB_COLS), F32)
                   + _nbytes((rows_q, LANES), F32) + _nbytes((s_new, QA_W + QB_W), F32))
              + 2 * _nbytes((SA_COLS, HEAD_DIM), BF16) + 2 * _nbytes((SB_COLS, HEAD_DIM), BF16)
              + 4 * _nbytes((rows_q, SA_COLS), F32))
    return pl.pallas_call(
        _attn_sample_kernel,
        grid=(db,),
        in_specs=[pl.BlockSpec((s_new, IN_W), lambda b: (b, 0)),
                  pl.BlockSpec(xblk, lambda b: (layer, b, 0, 0, 0, 0)),
                  pl.BlockSpec(yblk, lambda b: (layer, b, y_idx, 0, 0)),
                  pl.BlockSpec(xblk, lambda b: (layer, b, 0, 0, 0, 0)),
                  pl.BlockSpec(yblk, lambda b: (layer, b, y_idx, 0, 0)),
                  pl.BlockSpec(bblk, lambda b: (layer, b, 0, 0)),
                  pl.BlockSpec(bblk, lambda b: (layer, b, 0, 0)),
                  pl.BlockSpec((rows_q, SA_COLS), const),
                  pl.BlockSpec((rows_q, SB_COLS), const),
                  pl.BlockSpec((rows_q, LANES), const)],
        out_specs=pl.BlockSpec((s_new, QA_W + QB_W), lambda b: (b, 0)),
        out_shape=jax.ShapeDtypeStruct((db * s_new, QA_W + QB_W), F32),
        scratch_shapes=[pltpu.VMEM((SA_COLS, HEAD_DIM), BF16), pltpu.VMEM((SA_COLS, HEAD_DIM), BF16),
                        pltpu.VMEM((SB_COLS, HEAD_DIM), BF16), pltpu.VMEM((SB_COLS, HEAD_DIM), BF16),
                        pltpu.VMEM((rows_q, SA_COLS), F32)],
        compiler_params=pltpu.CompilerParams(dimension_semantics=("parallel",),
                                             vmem_limit_bytes=_vmem_limit(blocks)),
        name="mixers_sample",
    )(qkv, ax_k, cache_ak, ax_v, cache_av, bk, bv, mb_a, mb_b, sink_rows)


def _outproj_kernel(x_ref, ma_ref, mb_ref, wa_ref, wb_ref, gt_ref, o_ref):
    acc = jnp.dot(ma_ref[...].astype(BF16), wa_ref[...], preferred_element_type=F32)
    acc += jnp.dot(mb_ref[...].astype(BF16), wb_ref[...], preferred_element_type=F32)
    o_ref[...] = x_ref[...] + gt_ref[...] * acc.reshape(o_ref.shape)


def _out_projection(x3, mix_a, mix_b, col_a, col_b, layer, w_out_bf, mod3, bb, rb):
    nb, r, d = x3.shape
    tm = bb * rb
    tn = d
    rblocks = r // rb
    n_m = (nb // bb) * rblocks
    kh = QA_W
    blocks = 2 * (2 * _nbytes((tm, tn), F32) + _nbytes((tm, kh), mix_a.dtype) + _nbytes((tm, kh), mix_b.dtype)
                  + 2 * _nbytes((kh, tn), BF16) + _nbytes((bb, SUBLANES, tn), F32))
    return pl.pallas_call(
        _outproj_kernel,
        grid=(n_m, d // tn),
        in_specs=[pl.BlockSpec((bb, rb, tn), lambda i, j: (i // rblocks, i % rblocks, j)),
                  pl.BlockSpec((tm, kh), lambda i, j: (i, col_a)),
                  pl.BlockSpec((tm, kh), lambda i, j: (i, col_b)),
                  pl.BlockSpec((None, kh, tn), lambda i, j: (layer, 0, j)),
                  pl.BlockSpec((None, kh, tn), lambda i, j: (layer, 1, j)),
                  pl.BlockSpec((bb, 1, tn), lambda i, j: (i // rblocks, 0, 2 * (d // tn) + j))],
        out_specs=pl.BlockSpec((bb, rb, tn), lambda i, j: (i // rblocks, i % rblocks, j)),
        out_shape=jax.ShapeDtypeStruct(x3.shape, F32),
        compiler_params=pltpu.CompilerParams(dimension_semantics=("parallel", "parallel"),
                                             vmem_limit_bytes=_vmem_limit(blocks)),
        name="out_projection",
    )(x3, mix_a, mix_b, w_out_bf, w_out_bf, mod3)


def _ffn_kernel(x_ref, sc_ref, sh_ref, gt_ref, g_ref, wg_ref, wu_ref, wd_ref, wc_ref, bc_ref, hist_ref,
                o_ref, cs_ref, h_scr, gate_scr, up_scr, act_scr, carry_scr, *, rblocks, chunk_seqs, chunk_rows):
    i = pl.program_id(0)
    f = pl.program_id(1)
    n_f = pl.num_programs(1)
    bb, rb, d = x_ref.shape
    tm, tf = bb * rb, wg_ref.shape[1]

    @pl.when(f == 0)
    def _():
        _norm_modulate_rows(x_ref, g_ref, sc_ref, sh_ref, h_scr)
        o_ref[...] = jnp.zeros(o_ref.shape, F32)

    if rblocks > 1:
        @pl.when(i % rblocks == 0)
        def _():
            carry_scr[f] = hist_ref[...]
        history = lambda s0, n: carry_scr[f]
    else:
        history = lambda s0, n: hist_ref[s0:s0 + n]

    h = h_scr[...]
    gate_scr[0:SUBLANES, :] = jnp.zeros((SUBLANES, tf), F32)
    gate_scr[SUBLANES:, :] = jnp.dot(h, wg_ref[...], preferred_element_type=F32)
    up_scr[...] = jnp.dot(h, wu_ref[...], preferred_element_type=F32)

    n_rows = chunk_seqs * chunk_rows
    shape = (chunk_seqs, chunk_rows, tf)
    t = lax.broadcasted_iota(jnp.int32, shape, 1)
    for r0 in range(0, tm, n_rows):
        def shifted(s):
            lo = SUBLANES + r0 - s
            return gate_scr[lo:lo + n_rows, :].reshape(shape)
        g0, g1, g2 = shifted(0), shifted(1), shifted(2)
        if r0 % rb == 0:
            hist = history(r0 // rb, chunk_seqs)
            g1 = jnp.where(t >= 1, g1, hist[:, 1:2, :])
            g2 = jnp.where(t >= 2, g2, jnp.where(t == 0, hist[:, 0:1, :], hist[:, 1:2, :]))
        gc = bc_ref[...] + wc_ref[0:1, :] * g2 + wc_ref[1:2, :] * g1 + wc_ref[2:3, :] * g0
        act = jax.nn.gelu(gc, approximate=True) * up_scr[r0:r0 + n_rows, :].reshape(shape)
        act_scr[r0:r0 + n_rows, :] = act.reshape(n_rows, tf).astype(BF16)
        if (r0 + n_rows) % rb == 0:
            s0 = (r0 + n_rows) // rb - chunk_seqs
            tail = g0[:, chunk_rows - (CONV_W - 1):, :]
            cs_ref[s0:s0 + chunk_seqs] = tail
            if rblocks > 1:
                carry_scr[f] = tail
    o_ref[...] += jnp.dot(act_scr[...], wd_ref[...], preferred_element_type=F32).reshape(bb, rb, d)

    @pl.when(f == n_f - 1)
    def _():
        o_ref[...] = x_ref[...] + gt_ref[...] * o_ref[...]


FFN_CHUNK_ROWS = 64


def _conv_ffn(x3, mod3, layer, g_ffn, wg_bf, wu_bf, wd_bf, w_conv, b_conv, hist, bb, rb):
    nb, r, d = x3.shape
    f_pad = wg_bf.shape[2]
    tf = FF_TILE
    n_f = f_pad // tf
    tm = bb * rb
    rblocks = r // rb
    assert rblocks == 1 or bb == 1
    n_m = (nb // bb) * rblocks
    carry_slots = n_f if rblocks > 1 else 1
    chunk_rows = min(rb, FFN_CHUNK_ROWS)
    chunk_seqs = FFN_CHUNK_ROWS // chunk_rows
    assert rb % chunk_rows == 0 and bb % chunk_seqs == 0 and chunk_rows >= CONV_W - 1
    modmap = lambda k: (lambda i, f: (i // rblocks, 0, k))
    blocks = (2 * (2 * _nbytes((tm, d), F32) + 2 * _nbytes((d, tf), BF16) + _nbytes((tf, d), BF16)
                   + 2 * _nbytes((bb, SUBLANES, tf), F32) + 3 * _nbytes((bb, SUBLANES, d), F32))
              + _nbytes((tm, d), BF16) + carry_slots * _nbytes((bb, SUBLANES, tf), F32)
              + 3 * _nbytes((tm + SUBLANES, tf), F32))
    y, tails = pl.pallas_call(
        functools.partial(_ffn_kernel, rblocks=rblocks, chunk_seqs=chunk_seqs, chunk_rows=chunk_rows),
        grid=(n_m, n_f),
        in_specs=[pl.BlockSpec((bb, rb, d), lambda i, f: (i // rblocks, i % rblocks, 0)),
                  pl.BlockSpec((bb, 1, d), modmap(4)),
                  pl.BlockSpec((bb, 1, d), modmap(3)),
                  pl.BlockSpec((bb, 1, d), modmap(5)),
                  pl.BlockSpec((1, d), lambda i, f: (0, 0)),
                  pl.BlockSpec((None, d, tf), lambda i, f: (layer, 0, f)),
                  pl.BlockSpec((None, d, tf), lambda i, f: (layer, 0, f)),
                  pl.BlockSpec((None, tf, d), lambda i, f: (layer, f, 0)),
                  pl.BlockSpec((None, CONV_W, tf), lambda i, f: (layer, 0, f)),
                  pl.BlockSpec((None, 1, tf), lambda i, f: (layer, 0, f)),
                  pl.BlockSpec((bb, CONV_W - 1, tf), lambda i, f: (i // rblocks, 0, f))],
        out_specs=[pl.BlockSpec((bb, rb, d), lambda i, f: (i // rblocks, i % rblocks, 0)),
                   pl.BlockSpec((bb, CONV_W - 1, tf), lambda i, f: (i, 0, f))],
        out_shape=[jax.ShapeDtypeStruct(x3.shape, F32),
                   jax.ShapeDtypeStruct((n_m * bb, CONV_W - 1, f_pad), F32)],
        scratch_shapes=[pltpu.VMEM((tm, d), BF16),
                        pltpu.VMEM((tm + SUBLANES, tf), F32),
                        pltpu.VMEM((tm, tf), F32),
                        pltpu.VMEM((tm, tf), BF16),
                        pltpu.VMEM((carry_slots, bb, CONV_W - 1, tf), F32)],
        compiler_params=pltpu.CompilerParams(dimension_semantics=("arbitrary", "arbitrary"),
                                             vmem_limit_bytes=_vmem_limit(blocks)),
        name="conv_ffn",
    )(x3, mod3, mod3, mod3, g_ffn.reshape(1, d), wg_bf, wu_bf, wd_bf, w_conv, b_conv, hist)
    state = tails.reshape(nb // bb, rblocks, bb, CONV_W - 1, f_pad)[:, rblocks - 1]
    return y, state.reshape(nb, CONV_W - 1, f_pad)


def _pad_last(a, n):
    return jnp.pad(a, [(0, 0)] * (a.ndim - 1) + [(0, n - a.shape[-1])])


def _cast_pad_cols_kernel(x_ref, o_ref):
    n = x_ref.shape[1]
    o_ref[:, :n] = x_ref[...].astype(BF16)
    o_ref[:, n:] = jnp.zeros((o_ref.shape[0], o_ref.shape[1] - n), BF16)


def _cast_pad_cols(w, n_pad):
    depth, rows, n = w.shape
    tr = 256
    blocks = 2 * (_nbytes((tr, n), F32) + _nbytes((tr, n_pad), BF16))
    return pl.pallas_call(
        _cast_pad_cols_kernel,
        grid=(depth, rows // tr),
        in_specs=[pl.BlockSpec((None, tr, n), lambda l, i: (l, i, 0))],
        out_specs=pl.BlockSpec((None, tr, n_pad), lambda l, i: (l, i, 0)),
        out_shape=jax.ShapeDtypeStruct((depth, rows, n_pad), BF16),
        compiler_params=pltpu.CompilerParams(dimension_semantics=("parallel", "parallel"),
                                             vmem_limit_bytes=_vmem_limit(blocks)),
        name="cast_pad_cols",
    )(w)


def kernel(x_prompt, x_sample, c_prompt, c_sample, cache_a_k, cache_a_v, cache_b_k, cache_b_v, state_conv, rel_bias, w_ada, b_ada, g_attn, g_ffn, w_in, g_qn_a, g_kn_a, g_qn_b, g_kn_b, sinks, w_out, w_gate, w_up, w_conv, b_conv, w_down):
    depth = w_in.shape[0]
    batch, seq, d = x_prompt.shape
    db, s_new, _ = x_sample.shape
    d_ff = w_gate.shape[2]
    f_pad = -(-d_ff // FF_TILE) * FF_TILE
    la, lb = cache_a_k.shape[2], cache_b_k.shape[2]
    assert batch == 1 and seq % CHUNK_A == 0 and s_new == SUBLANES

    pb = _bias_by_distance(rel_bias, max(WIN_A + 1, la + s_new)) * LOG2E
    bias_a, bias_b = _prompt_bias_tables(pb)
    mb_a, mb_b = _sample_tables(pb, s_new, la, lb)
    sinks2 = sinks.astype(F32) * LOG2E

    c_all = jnp.concatenate([c_prompt, jnp.zeros((SUBLANES - batch, d), F32), c_sample], axis=0)

    ones = jnp.ones((HEAD_DIM,), F32)
    flag = jnp.concatenate([jnp.ones((2 * QA_W,), F32), jnp.zeros((QA_W,), F32), jnp.ones((QB_W + KB_W,), F32),
                            jnp.zeros((KB_W,), F32)]).reshape(1, IN_W)

    w_in_bf = w_in.astype(BF16)
    w_out_bf = w_out.astype(BF16)
    wg_bf = _cast_pad_cols(w_gate, f_pad)
    wu_bf = _cast_pad_cols(w_up, f_pad)
    wd_bf = jnp.pad(w_down.astype(BF16), ((0, 0), (0, f_pad - d_ff), (0, 0)))
    wc = _pad_last(w_conv, f_pad)
    bc = _pad_last(b_conv, f_pad).reshape(depth, 1, f_pad)
    mod_all = _modulation(c_all, w_ada, b_ada)

    xp = x_prompt
    xs = x_sample
    hist_p = jnp.zeros((batch, CONV_W - 1, f_pad), F32)
    outs = [[] for _ in range(10)]
    for l in range(depth):
        gain = jnp.concatenate([jnp.tile(g_qn_a[l], N_HEADS_A), jnp.tile(g_kn_a[l], N_HEADS_A),
                                jnp.tile(ones, N_HEADS_A), jnp.tile(g_qn_b[l], N_HEADS_B),
                                jnp.tile(g_kn_b[l], N_KV_B), jnp.tile(ones, N_KV_B)]).reshape(1, IN_W)
        sink_rows = jnp.broadcast_to(jnp.repeat(sinks2[l], s_new)[:, None], (N_HEADS_B * s_new, LANES))
        mod_p = mod_all[l, 0:batch].reshape(batch, 1, 6 * d)
        mod_s = mod_all[l, SUBLANES:].reshape(db, 1, 6 * d)

        qkv_p = _projection(xp, mod_p, l, g_attn[l], w_in_bf, gain, flag, 1, 1024)
        mix_pa = _attn_a_prompt(qkv_p, bias_a)
        mix_pb = _attn_b_prompt(qkv_p, bias_b, sinks2[l])
        xp = _out_projection(xp, mix_pa, mix_pb, 0, 0, l, w_out_bf, mod_p, 1, 512)
        xp, conv_p = _conv_ffn(xp, mod_p, l, g_ffn[l], wg_bf, wu_bf, wd_bf, wc, bc, hist_p, 1, 512)
        keep_a, keep_b = min(WIN_A, seq), min(WIN_B, seq)
        outs[0].append(qkv_p[seq - keep_a:, COL_KA * LANES:COL_KA * LANES + QA_W].reshape(batch, keep_a, N_HEADS_A, HEAD_DIM))
        outs[1].append(qkv_p[seq - keep_a:, COL_VA * LANES:COL_VA * LANES + QA_W].reshape(batch, keep_a, N_HEADS_A, HEAD_DIM))
        outs[2].append(qkv_p[seq - keep_b:, COL_KB * LANES:COL_KB * LANES + KB_W].reshape(batch, keep_b, N_KV_B, HEAD_DIM))
        outs[3].append(qkv_p[seq - keep_b:, COL_VB * LANES:COL_VB * LANES + KB_W].reshape(batch, keep_b, N_KV_B, HEAD_DIM))
        outs[4].append(conv_p[:, :, :d_ff])

        qkv_s = _projection(xs, mod_s, l, g_attn[l], w_in_bf, gain, flag, db, s_new)
        mix_s = _attn_sample(qkv_s, l, cache_a_k, cache_a_v, cache_b_k, cache_b_v,
                             mb_a, mb_b, sink_rows, s_new)
        xs = _out_projection(xs, mix_s, mix_s, 0, 1, l, w_out_bf, mod_s, db // 2, s_new)
        xs, conv_s = _conv_ffn(xs, mod_s, l, g_ffn[l], wg_bf, wu_bf, wd_bf, wc, bc,
                               _pad_last(state_conv[l], f_pad), db // 2, s_new)
        outs[5].append(qkv_s[:, COL_KA * LANES:COL_KA * LANES + QA_W].reshape(db, s_new, N_HEADS_A, HEAD_DIM))
        outs[6].append(qkv_s[:, COL_VA * LANES:COL_VA * LANES + QA_W].reshape(db, s_new, N_HEADS_A, HEAD_DIM))
        outs[7].append(qkv_s[:, COL_KB * LANES:COL_KB * LANES + KB_W].reshape(db, s_new, N_KV_B, HEAD_DIM))
        outs[8].append(qkv_s[:, COL_VB * LANES:COL_VB * LANES + KB_W].reshape(db, s_new, N_KV_B, HEAD_DIM))
        outs[9].append(conv_s[:, :, :d_ff])

    return (xp, xs) + tuple(jnp.stack(o) for o in outs)
```

```python
import functools
import math

import numpy as np
import jax
import jax.numpy as jnp
from jax import lax
from jax.experimental import pallas as pl
from jax.experimental.pallas import tpu as pltpu

F32 = jnp.float32
BF16 = jnp.bfloat16

LANES = 128
SUBLANES = 8
VMEM_BYTES_V7X = 64 * 1024 * 1024

HEAD_DIM = 128
N_HEADS_A = 8
N_HEADS_B = 8
N_KV_B = 2
GROUP_B = N_HEADS_B // N_KV_B
A_BRANCHES = ((128, 1), (512, 4), (2048, 16))
WIN_A = 2048
WIN_B = 128
BLOCK = 128
N_BUCKETS = 32
MAX_DISTANCE = 2048
CONV_W = 3
EPS = 1e-6
NEG = -1e30
SCALE = HEAD_DIM ** -0.5
LOG2E = math.log2(math.e)
QK_SCALE_LOG2 = SCALE * LOG2E

QA_W = N_HEADS_A * HEAD_DIM
QB_W = N_HEADS_B * HEAD_DIM
KB_W = N_KV_B * HEAD_DIM
IN_W = 3 * QA_W + QB_W + 2 * KB_W
COL_QA, COL_KA, COL_VA = 0, QA_W // LANES, 2 * QA_W // LANES
COL_QB = 3 * QA_W // LANES
COL_KB = COL_QB + QB_W // LANES
COL_VB = COL_KB + KB_W // LANES

CHUNK_A = WIN_A
CHUNK_B = 1024
FF_TILE = 512


def _vmem_limit(block_bytes):
    return int(min(VMEM_BYTES_V7X - (4 << 20), block_bytes + (12 << 20)))


def _nbytes(shape, dtype):
    return int(np.prod(shape)) * jnp.dtype(dtype).itemsize


def _t5_bucket_np(dist):
    dist = np.maximum(dist, 0)
    max_exact = N_BUCKETS // 2
    ratio = np.log(np.maximum(dist, 1).astype(np.float64) / max_exact) / math.log(MAX_DISTANCE / max_exact)
    large = max_exact + (ratio * (N_BUCKETS - max_exact)).astype(np.int32)
    large = np.minimum(large, N_BUCKETS - 1)
    return np.where(dist < max_exact, dist, large).astype(np.int32)


def _bias_by_distance(rel_bias, n):
    buckets = _t5_bucket_np(np.arange(n))
    assert np.all(np.diff(buckets) >= 0)
    runs = np.bincount(buckets, minlength=N_BUCKETS)
    tab = rel_bias.astype(F32).T
    return jnp.concatenate([jnp.broadcast_to(tab[:, k:k + 1], (tab.shape[0], int(runs[k])))
                            for k in range(N_BUCKETS) if runs[k] > 0], axis=1)


def _banded_table(vals):
    hh, n = vals.shape
    period = 4 * BLOCK
    w = jnp.concatenate([jnp.full((hh, BLOCK - 1), NEG, F32), vals,
                         jnp.full((hh, period - (BLOCK - 1) - n), NEG, F32)], axis=1)
    skew = jnp.tile(w, (1, BLOCK + 1))[:, :BLOCK * (period + 1)].reshape(hh, BLOCK, period + 1)
    return jnp.flip(skew[:, :, :2 * BLOCK], axis=2)


def _prompt_bias_tables(pb):
    bias_a = jnp.stack([_banded_table(pb[:N_HEADS_A, 0:BLOCK * dil + 1:dil]) for _, dil in A_BRANCHES])
    bias_b = _banded_table(pb[N_HEADS_A:, 0:WIN_B])
    return bias_a, bias_b.reshape(N_KV_B, GROUP_B, BLOCK, 2 * BLOCK)


DIL_MAX = A_BRANCHES[-1][1]
SA_Y_POS = A_BRANCHES[1][0]
SA_X_M = (WIN_A - SA_Y_POS) // DIL_MAX
SA_X_SPLIT = 3
SA_X_COLS = SA_X_M * SUBLANES * N_HEADS_A
SA_Y_COLS = SA_Y_POS * N_HEADS_A
SA_CHUNK = 768
SA_NEW_COLS = -(SA_X_COLS + SA_Y_COLS + N_HEADS_A * SUBLANES) % SA_CHUNK + N_HEADS_A * SUBLANES
SA_COLS = SA_X_COLS + SA_Y_COLS + SA_NEW_COLS
assert SA_COLS % SA_CHUNK == 0 and SA_NEW_COLS % (2 * SUBLANES) == 0
SB_CACHE_COLS = WIN_B * N_KV_B
SB_NEW_COLS = LANES
SB_COLS = SB_CACHE_COLS + SB_NEW_COLS


def _sample_columns(s_new, la, lb):
    m, res, h = np.meshgrid(np.arange(SA_X_M), np.arange(SUBLANES), np.arange(N_HEADS_A), indexing="ij")
    pos_x, head_x = (DIL_MAX * m + res).reshape(-1), h.reshape(-1)
    p, h = np.meshgrid(np.arange(la - SA_Y_POS, la), np.arange(N_HEADS_A), indexing="ij")
    pos_y, head_y = p.reshape(-1), h.reshape(-1)
    h, t = np.meshgrid(np.arange(N_HEADS_A), np.arange(s_new), indexing="ij")
    pad = np.full((SA_NEW_COLS - N_HEADS_A * s_new,), -1)
    pos_a = np.concatenate([pos_x, pos_y, la + t.reshape(-1), pad])
    head_a = np.concatenate([head_x, head_y, h.reshape(-1), pad])
    p, g = np.meshgrid(np.arange(lb), np.arange(N_KV_B), indexing="ij")
    g2, t = np.meshgrid(np.arange(N_KV_B), np.arange(s_new), indexing="ij")
    pad = np.full((SB_NEW_COLS - N_KV_B * s_new,), -1)
    pos_b = np.concatenate([p.reshape(-1), lb + t.reshape(-1), pad])
    head_b = np.concatenate([g.reshape(-1), g2.reshape(-1), pad])
    return pos_a, head_a, pos_b, head_b


def _query_key_bias(pbh, length, s_new):
    n = length + s_new
    rev = jnp.pad(jnp.flip(pbh[:, :n], axis=1), ((0, 0), (0, s_new - 1)))
    return jnp.stack([rev[:, s_new - 1 - i:s_new - 1 - i + n] for i in range(s_new)], axis=1)


def _per_key_head(a, n_key_heads, key_head_of, head_major):
    hh = a.shape[0]
    own = jnp.asarray(np.array([[key_head_of(h) == g for g in range(n_key_heads)] for h in range(hh)]))
    if head_major:
        out = jnp.where(own[:, None, :, None], a[:, :, None, :], NEG)
    else:
        out = jnp.where(own[:, None, None, :], a[:, :, :, None], NEG)
    return out.reshape(hh * a.shape[1], -1)


def _sample_tables(pb, s_new, la, lb):
    assert la == WIN_A and lb == WIN_B and s_new == SUBLANES
    pos_a, head_a, pos_b, head_b = _sample_columns(s_new, la, lb)
    qh, qi = np.meshgrid(np.arange(N_HEADS_A), np.arange(s_new), indexing="ij")
    qh, qi = qh.reshape(-1, 1), qi.reshape(-1, 1)

    dist = la + qi - pos_a[None, :]
    real = (pos_a >= 0)[None, :] & (dist >= 0) & (head_a[None, :] == qh)
    count = np.zeros(dist.shape, np.float32)
    for win, dil in A_BRANCHES:
        count += (real & (dist % dil == 0) & (dist <= win)).astype(np.float32)
    a = _query_key_bias(pb[:N_HEADS_A], la, s_new)
    same = lambda h: h
    a_x = a[:, :, :la - SA_Y_POS].reshape(N_HEADS_A, s_new, SA_X_M, DIL_MAX)[:, :, :, :SUBLANES]
    mb_a = jnp.concatenate([
        _per_key_head(a_x.reshape(N_HEADS_A, s_new, -1), N_HEADS_A, same, False),
        _per_key_head(a[:, :, la - SA_Y_POS:la], N_HEADS_A, same, False),
        _per_key_head(a[:, :, la:], N_HEADS_A, same, True),
        jnp.full((N_HEADS_A * s_new, SA_NEW_COLS - N_HEADS_A * s_new), NEG, F32)], axis=1)
    mb_a = jnp.where(jnp.asarray(count > 0), mb_a + jnp.asarray(np.log2(np.maximum(count, 1.0))), NEG)

    dist_b = lb + qi - pos_b[None, :]
    valid_b = (pos_b >= 0)[None, :] & (dist_b >= 0) & (dist_b < WIN_B) & (head_b[None, :] == qh // GROUP_B)
    b = _query_key_bias(pb[N_HEADS_A:], lb, s_new)
    group = lambda h: h // GROUP_B
    mb_b = jnp.concatenate([
        _per_key_head(b[:, :, :lb], N_KV_B, group, False),
        _per_key_head(b[:, :, lb:], N_KV_B, group, True),
        jnp.full((N_HEADS_B * s_new, SB_NEW_COLS - N_KV_B * s_new), NEG, F32)], axis=1)
    mb_b = jnp.where(jnp.asarray(valid_b), mb_b, NEG)
    return mb_a, mb_b


def _split_bf16(a):
    hi = a.astype(BF16)
    lo = (a - hi.astype(F32)).astype(BF16)
    return hi, lo


def _mod_kernel(c_ref, w_ref, b_ref, o_ref):
    c = c_ref[...]
    s_hi, s_lo = _split_bf16(c * jax.nn.sigmoid(c))
    w_hi, w_lo = _split_bf16(w_ref[...])
    acc = jnp.dot(s_hi, w_hi, preferred_element_type=F32)
    acc += jnp.dot(s_hi, w_lo, preferred_element_type=F32)
    acc += jnp.dot(s_lo, w_hi, preferred_element_type=F32)
    o_ref[...] = acc + b_ref[...]


def _modulation(c_all, w_ada, b_ada):
    m, d = c_all.shape
    depth, _, n = w_ada.shape
    tn = 1024
    blocks = 2 * (_nbytes((d, tn), F32) + _nbytes((m, tn), F32)) + _nbytes((m, d), F32) * 2 + 2 * _nbytes((d, tn), BF16)
    return pl.pallas_call(
        _mod_kernel,
        grid=(depth, n // tn),
        in_specs=[pl.BlockSpec((m, d), lambda l, j: (0, 0)),
                  pl.BlockSpec((None, d, tn), lambda l, j: (l, 0, j)),
                  pl.BlockSpec((None, 1, tn), lambda l, j: (l, 0, j))],
        out_specs=pl.BlockSpec((None, m, tn), lambda l, j: (l, 0, j)),
        out_shape=jax.ShapeDtypeStruct((depth, m, n), F32),
        compiler_params=pltpu.CompilerParams(dimension_semantics=("parallel", "parallel"),
                                             vmem_limit_bytes=_vmem_limit(blocks)),
        name="modulation",
    )(c_all, w_ada, b_ada.reshape(depth, 1, n))


def _norm_modulate(x, g, sc, sh):
    ms = jnp.mean(x * x, axis=-1, keepdims=True)
    return (x * lax.rsqrt(ms + EPS) * g) * (1.0 + sc) + sh


NORM_ROWS = 16


def _norm_modulate_rows(x_ref, g_ref, sc_ref, sh_ref, h_scr):
    bb, rb, d = x_ref.shape
    cr = min(rb, NORM_ROWS)
    cs = NORM_ROWS // cr
    for r0 in range(0, bb * rb, NORM_ROWS):
        s, t0 = r0 // rb, r0 % rb
        h = _norm_modulate(x_ref[s:s + cs, t0:t0 + cr, :], g_ref[...], sc_ref[s:s + cs], sh_ref[s:s + cs])
        h_scr[r0:r0 + NORM_ROWS, :] = h.reshape(NORM_ROWS, d).astype(BF16)


def _dot_nt(a, b):
    return lax.dot_general(a, b, (((1,), (1,)), ((), ())), preferred_element_type=F32)


def _proj_kernel(x_ref, sc_ref, sh_ref, g_ref, w_ref, gain_ref, flag_ref, o_ref, h_scr, acc_scr):
    @pl.when(pl.program_id(1) == 0)
    def _():
        _norm_modulate_rows(x_ref, g_ref, sc_ref, sh_ref, h_scr)

    acc_scr[...] = jnp.dot(h_scr[...], w_ref[...], preferred_element_type=F32)
    tm, tn = acc_scr.shape
    for r0 in range(0, tm, 64):
        for c in range(tn // LANES):
            sl = slice(c * LANES, (c + 1) * LANES)
            blk = acc_scr[r0:r0 + 64, sl]
            ms = jnp.mean(blk * blk, axis=-1, keepdims=True)
            nrm = blk * lax.rsqrt(ms + EPS) * gain_ref[:, sl]
            o_ref[r0:r0 + 64, sl] = jnp.where(flag_ref[:, sl] > 0.0, nrm, blk)


def _projection(x3, mod3, layer, g_attn, w_in_bf, gain, flag, bb, rb):
    nb, r, d = x3.shape
    tm = bb * rb
    tn = 768
    n_m = (nb // bb) * (r // rb)
    rblocks = r // rb
    xmap = lambda i, j: (i // rblocks, i % rblocks, 0)
    blocks = (2 * (_nbytes((tm, d), F32) + _nbytes((d, tn), BF16) + _nbytes((tm, tn), F32))
              + _nbytes((tm, d), BF16) + 8 * _nbytes((bb, SUBLANES, d), F32) + _nbytes((tm, tn), F32))
    return pl.pallas_call(
        _proj_kernel,
        grid=(n_m, IN_W // tn),
        in_specs=[pl.BlockSpec((bb, rb, d), xmap),
                  pl.BlockSpec((bb, 1, d), lambda i, j: (i // rblocks, 0, 1)),
                  pl.BlockSpec((bb, 1, d), lambda i, j: (i // rblocks, 0, 0)),
                  pl.BlockSpec((1, d), lambda i, j: (0, 0)),
                  pl.BlockSpec((None, d, tn), lambda i, j: (layer, 0, j)),
                  pl.BlockSpec((1, tn), lambda i, j: (0, j)),
                  pl.BlockSpec((1, tn), lambda i, j: (0, j))],
        out_specs=pl.BlockSpec((tm, tn), lambda i, j: (i, j)),
        out_shape=jax.ShapeDtypeStruct((nb * r, IN_W), F32),
        scratch_shapes=[pltpu.VMEM((tm, d), BF16), pltpu.VMEM((tm, tn), F32)],
        compiler_params=pltpu.CompilerParams(dimension_semantics=("parallel", "arbitrary"),
                                             vmem_limit_bytes=_vmem_limit(blocks)),
        name="projection",
    )(x3, mod3, mod3, g_attn.reshape(1, d), w_in_bf, gain, flag)


def _rows(ref, start, dil):
    if dil == 1:
        return ref[start:start + BLOCK, :]
    return ref[pl.ds(start, BLOCK, stride=dil), :]


def _attn_a_prompt_kernel(q_ref, kc_ref, kp_ref, vc_ref, vp_ref, bias_ref, o_ref, num_scr, m_scr, s_scr):
    first_chunk = pl.program_id(0) == 0
    chunk = q_ref.shape[0]
    col = lax.broadcasted_iota(jnp.int32, (BLOCK, 2 * BLOCK), 1)
    ones = jnp.ones((2 * BLOCK, LANES), BF16)
    for bi, (_, dil) in enumerate(A_BRANCHES):
        nsub = chunk // (BLOCK * dil)
        bias = bias_ref[bi]
        for r in range(dil):
            last = r + dil * BLOCK * (nsub - 1)
            k_prev = _rows(kp_ref, last, dil).astype(BF16)
            v_prev = _rows(vp_ref, last, dil).astype(BF16)
            for ub in range(nsub):
                start = r + dil * BLOCK * ub
                q = (_rows(q_ref, start, dil) * QK_SCALE_LOG2).astype(BF16)
                k_cur = _rows(kc_ref, start, dil).astype(BF16)
                v_cur = _rows(vc_ref, start, dil).astype(BF16)
                z = _dot_nt(q, jnp.concatenate([k_prev, k_cur], axis=0)) + bias
                if ub == 0:
                    z = jnp.where(jnp.logical_and(first_chunk, col < BLOCK), NEG, z)
                m = jnp.max(z, axis=-1, keepdims=True)
                e = jnp.exp2(z - m).astype(BF16)
                v_ext = jnp.concatenate([jnp.concatenate([v_prev, v_cur], axis=0), ones], axis=1)
                num = jnp.dot(e, v_ext, preferred_element_type=F32)
                if dil == 1:
                    dst = (bi, slice(start, start + BLOCK), slice(None))
                else:
                    dst = (bi, pl.ds(start, BLOCK, stride=dil), slice(None))
                num_scr[dst] = num[:, :HEAD_DIM]
                m_scr[dst] = jnp.broadcast_to(m, (BLOCK, LANES))
                s_scr[dst] = num[:, HEAD_DIM:]
                k_prev, v_prev = k_cur, v_cur
    m_all = jnp.maximum(jnp.maximum(m_scr[0], m_scr[1]), m_scr[2])
    top = jnp.zeros_like(m_all)
    bot = jnp.zeros_like(m_all)
    for bi in range(len(A_BRANCHES)):
        w = jnp.exp2(m_scr[bi] - m_all)
        top += w * num_scr[bi]
        bot += w * s_scr[bi]
    o_ref[...] = (top / bot).astype(o_ref.dtype)


def _attn_a_prompt(qkv, bias_a):
    t = qkv.shape[0]
    n_chunks = t // CHUNK_A
    blk = (CHUNK_A, HEAD_DIM)
    prev = lambda b: jnp.maximum(b - 1, 0)
    blocks = (2 * (5 * _nbytes(blk, F32) + _nbytes((3, BLOCK, 2 * BLOCK), F32) + _nbytes(blk, BF16))
              + 9 * _nbytes(blk, F32))
    return pl.pallas_call(
        _attn_a_prompt_kernel,
        grid=(n_chunks, N_HEADS_A),
        in_specs=[pl.BlockSpec(blk, lambda b, h: (b, COL_QA + h)),
                  pl.BlockSpec(blk, lambda b, h: (b, COL_KA + h)),
                  pl.BlockSpec(blk, lambda b, h: (prev(b), COL_KA + h)),
                  pl.BlockSpec(blk, lambda b, h: (b, COL_VA + h)),
                  pl.BlockSpec(blk, lambda b, h: (prev(b), COL_VA + h)),
                  pl.BlockSpec((3, None, BLOCK, 2 * BLOCK), lambda b, h: (0, h, 0, 0))],
        out_specs=pl.BlockSpec(blk, lambda b, h: (b, h)),
        out_shape=jax.ShapeDtypeStruct((t, QA_W), BF16),
        scratch_shapes=[pltpu.VMEM((3, CHUNK_A, HEAD_DIM), F32)] * 3,
        compiler_params=pltpu.CompilerParams(dimension_semantics=("parallel", "parallel"),
                                             vmem_limit_bytes=_vmem_limit(blocks)),
        name="mixer_a_prompt",
    )(qkv, qkv, qkv, qkv, qkv, bias_a)


def _attn_b_prompt_kernel(sink_ref, q_ref, kc_ref, kp_ref, vc_ref, vp_ref, bias_ref, o_ref):
    first_chunk = pl.program_id(0) == 0
    g = pl.program_id(1)
    col = lax.broadcasted_iota(jnp.int32, (BLOCK, 2 * BLOCK), 1)
    k_prev = kp_ref[...].astype(BF16)
    v_prev = vp_ref[...].astype(BF16)
    for blk in range(q_ref.shape[0] // BLOCK):
        rows = slice(blk * BLOCK, (blk + 1) * BLOCK)
        k_cur = kc_ref[rows, :].astype(BF16)
        v_cur = vc_ref[rows, :].astype(BF16)
        kcat = jnp.concatenate([k_prev, k_cur], axis=0)
        vcat = jnp.concatenate([v_prev, v_cur], axis=0)
        for j in range(GROUP_B):
            cols = slice(j * HEAD_DIM, (j + 1) * HEAD_DIM)
            sink = sink_ref[g * GROUP_B + j]
            z = _dot_nt((q_ref[rows, cols] * QK_SCALE_LOG2).astype(BF16), kcat) + bias_ref[j]
            if blk == 0:
                z = jnp.where(jnp.logical_and(first_chunk, col < BLOCK), NEG, z)
            m = jnp.maximum(jnp.max(z, axis=-1, keepdims=True), sink)
            e = jnp.exp2(z - m)
            denom = jnp.sum(e, axis=-1, keepdims=True) + jnp.exp2(sink - m)
            num = jnp.dot(e.astype(BF16), vcat, preferred_element_type=F32)
            o_ref[rows, cols] = (num / denom).astype(o_ref.dtype)
        k_prev, v_prev = k_cur, v_cur


def _attn_b_prompt(qkv, bias_b, sinks):
    t = qkv.shape[0]
    n_chunks = t // CHUNK_B
    per = CHUNK_B // BLOCK
    qblk = (CHUNK_B, GROUP_B * HEAD_DIM)
    kblk = (CHUNK_B, HEAD_DIM)
    pblk = (BLOCK, HEAD_DIM)
    qcol = COL_QB // GROUP_B
    prev = lambda b: jnp.maximum(b * per - 1, 0)
    blocks = 2 * (_nbytes(qblk, F32) + 2 * _nbytes(kblk, F32) + 2 * _nbytes(pblk, F32)
                  + _nbytes((GROUP_B, BLOCK, 2 * BLOCK), F32) + _nbytes(qblk, BF16))
    return pl.pallas_call(
        _attn_b_prompt_kernel,
        grid=(n_chunks, N_KV_B),
        in_specs=[pl.BlockSpec(memory_space=pltpu.SMEM),
                  pl.BlockSpec(qblk, lambda b, g: (b, qcol + g)),
                  pl.BlockSpec(kblk, lambda b, g: (b, COL_KB + g)),
                  pl.BlockSpec(pblk, lambda b, g: (prev(b), COL_KB + g)),
                  pl.BlockSpec(kblk, lambda b, g: (b, COL_VB + g)),
                  pl.BlockSpec(pblk, lambda b, g: (prev(b), COL_VB + g)),
                  pl.BlockSpec((None, GROUP_B, BLOCK, 2 * BLOCK), lambda b, g: (g, 0, 0, 0))],
        out_specs=pl.BlockSpec(qblk, lambda b, g: (b, g)),
        out_shape=jax.ShapeDtypeStruct((t, QB_W), BF16),
        compiler_params=pltpu.CompilerParams(dimension_semantics=("parallel", "parallel"),
                                             vmem_limit_bytes=_vmem_limit(blocks)),
        name="mixer_b_prompt",
    )(sinks.astype(F32), qkv, qkv, qkv, qkv, qkv, bias_b)


def _heads_to_rows(x, n_heads):
    return jnp.concatenate([x[:, h * HEAD_DIM:(h + 1) * HEAD_DIM] for h in range(n_heads)], axis=0)


def _rows_to_heads(x, n_heads):
    s = x.shape[0] // n_heads
    return jnp.concatenate([x[h * s:(h + 1) * s, :] for h in range(n_heads)], axis=1)


def _attn_sample_kernel(qkv_ref, *refs):
    kx_refs, ky_ref = refs[0:SA_X_SPLIT], refs[SA_X_SPLIT]
    vx_refs, vy_ref = refs[SA_X_SPLIT + 1:2 * SA_X_SPLIT + 1], refs[2 * SA_X_SPLIT + 1]
    (kb_ref, vb_ref, mba_ref, mbb_ref, sink_ref, o_ref,
     ka_scr, va_scr, kb_scr, vb_scr, z_scr) = refs[2 * SA_X_SPLIT + 2:]

    def new_rows(col0, n_heads, n_rows):
        new = _heads_to_rows(qkv_ref[:, col0 * LANES:(col0 + n_heads) * LANES], n_heads)
        return jnp.concatenate([new, jnp.zeros((n_rows - new.shape[0], HEAD_DIM), F32)], axis=0).astype(BF16)

    def fill_a(scr, x_refs, y_ref, col0):
        part = SA_X_COLS // SA_X_SPLIT
        for s, x_ref in enumerate(x_refs):
            scr[s * part:(s + 1) * part, :] = x_ref[...].reshape(part, HEAD_DIM).astype(BF16)
        scr[SA_X_COLS:SA_X_COLS + SA_Y_COLS, :] = y_ref[...].reshape(SA_Y_COLS, HEAD_DIM).astype(BF16)
        scr[SA_X_COLS + SA_Y_COLS:SA_COLS, :] = new_rows(col0, N_HEADS_A, SA_NEW_COLS)

    def fill_b(scr, c_ref, col0):
        scr[0:SB_CACHE_COLS, :] = c_ref[...].astype(BF16)
        scr[SB_CACHE_COLS:SB_COLS, :] = new_rows(col0, N_KV_B, SB_NEW_COLS)

    def queries(col0, n_heads):
        q = _heads_to_rows(qkv_ref[:, col0 * LANES:(col0 + n_heads) * LANES], n_heads)
        return (q * QK_SCALE_LOG2).astype(BF16)

    fill_a(ka_scr, kx_refs, ky_ref, COL_KA)
    fill_a(va_scr, vx_refs, vy_ref, COL_VA)
    fill_b(kb_scr, kb_ref, COL_KB)
    fill_b(vb_scr, vb_ref, COL_VB)

    qa = queries(COL_QA, N_HEADS_A)
    m = None
    for c0 in range(0, SA_COLS, SA_CHUNK):
        z = _dot_nt(qa, ka_scr[c0:c0 + SA_CHUNK, :]) + mba_ref[:, c0:c0 + SA_CHUNK]
        z_scr[:, c0:c0 + SA_CHUNK] = z
        zmax = jnp.max(z, axis=-1, keepdims=True)
        m = zmax if m is None else jnp.maximum(m, zmax)
    denom, oa = 0.0, 0.0
    for c0 in range(0, SA_COLS, SA_CHUNK):
        p = jnp.exp2(z_scr[:, c0:c0 + SA_CHUNK] - m)
        denom = denom + jnp.sum(p, axis=-1, keepdims=True)
        oa = oa + jnp.dot(p.astype(BF16), va_scr[c0:c0 + SA_CHUNK, :], preferred_element_type=F32)
    o_ref[:, 0:QA_W] = _rows_to_heads(oa / denom, N_HEADS_A)

    zb = _dot_nt(queries(COL_QB, N_HEADS_B), kb_scr[...]) + mbb_ref[...]
    sink = sink_ref[:, 0:1]
    mb = jnp.maximum(jnp.max(zb, axis=-1, keepdims=True), sink)
    eb = jnp.exp2(zb - mb)
    denom_b = jnp.sum(eb, axis=-1, keepdims=True) + jnp.exp2(sink - mb)
    ob = jnp.dot(eb.astype(BF16), vb_scr[...], preferred_element_type=F32) / denom_b
    o_ref[:, QA_W:QA_W + QB_W] = _rows_to_heads(ob, N_HEADS_B)


def _attn_sample(qkv, layer, cache_ak, cache_av, cache_bk, cache_bv, mb_a, mb_b, sink_rows, s_new):
    depth, db, la = cache_ak.shape[0], cache_ak.shape[1], cache_ak.shape[2]
    lb = cache_bk.shape[2]
    ax_k = cache_ak.reshape(depth, db, la // DIL_MAX, DIL_MAX, N_HEADS_A, HEAD_DIM)
    ax_v = cache_av.reshape(depth, db, la // DIL_MAX, DIL_MAX, N_HEADS_A, HEAD_DIM)
    bk = cache_bk.reshape(depth, db, lb * N_KV_B, HEAD_DIM)
    bv = cache_bv.reshape(depth, db, lb * N_KV_B, HEAD_DIM)
    xblk = (None, None, SA_X_M // SA_X_SPLIT, SUBLANES, N_HEADS_A, HEAD_DIM)
    xspecs = [pl.BlockSpec(xblk, functools.partial(lambda b, s: (layer, b, s, 0, 0, 0), s=s))
              for s in range(SA_X_SPLIT)]
    yblk = (None, None, SA_Y_POS, N_HEADS_A, HEAD_DIM)
    bblk = (None, None, SB_CACHE_COLS, HEAD_DIM)
    y_idx = (la - SA_Y_POS) // SA_Y_POS
    rows_q = N_HEADS_A * s_new
    const = lambda b: (0, 0)
    blocks = (2 * (_nbytes((s_new, IN_W), F32) + 2 * _nbytes((SA_X_COLS + SA_Y_COLS, HEAD_DIM), F32)
                   + 2 * _nbytes((SB_CACHE_COLS, HEAD_DIM), F32)
                   + _nbytes((rows_q, SA_COLS), F32) + _nbytes((rows_q, SB_COLS), F32)
                   + _nbytes((rows_q, LANES), F32) + _nbytes((s_new, QA_W + QB_W), F32))
              + 2 * _nbytes((SA_COLS, HEAD_DIM), BF16) + 2 * _nbytes((SB_COLS, HEAD_DIM), BF16)
              + 4 * _nbytes((rows_q, SA_COLS), F32))
    return pl.pallas_call(
        _attn_sample_kernel,
        grid=(db,),
        in_specs=[pl.BlockSpec((s_new, IN_W), lambda b: (b, 0)),
                  *xspecs,
                  pl.BlockSpec(yblk, lambda b: (layer, b, y_idx, 0, 0)),
                  *xspecs,
                  pl.BlockSpec(yblk, lambda b: (layer, b, y_idx, 0, 0)),
                  pl.BlockSpec(bblk, lambda b: (layer, b, 0, 0)),
                  pl.BlockSpec(bblk, lambda b: (layer, b, 0, 0)),
                  pl.BlockSpec((rows_q, SA_COLS), const),
                  pl.BlockSpec((rows_q, SB_COLS), const),
                  pl.BlockSpec((rows_q, LANES), const)],
        out_specs=pl.BlockSpec((s_new, QA_W + QB_W), lambda b: (b, 0)),
        out_shape=jax.ShapeDtypeStruct((db * s_new, QA_W + QB_W), F32),
        scratch_shapes=[pltpu.VMEM((SA_COLS, HEAD_DIM), BF16), pltpu.VMEM((SA_COLS, HEAD_DIM), BF16),
                        pltpu.VMEM((SB_COLS, HEAD_DIM), BF16), pltpu.VMEM((SB_COLS, HEAD_DIM), BF16),
                        pltpu.VMEM((rows_q, SA_COLS), F32)],
        compiler_params=pltpu.CompilerParams(dimension_semantics=("parallel",),
                                             vmem_limit_bytes=_vmem_limit(blocks)),
        name="mixers_sample",
    )(qkv, *([ax_k] * SA_X_SPLIT), cache_ak, *([ax_v] * SA_X_SPLIT), cache_av, bk, bv, mb_a, mb_b, sink_rows)


def _outproj_kernel(x_ref, ma_ref, mb_ref, wa_ref, wb_ref, gt_ref, o_ref):
    acc = jnp.dot(ma_ref[...].astype(BF16), wa_ref[...], preferred_element_type=F32)
    acc += jnp.dot(mb_ref[...].astype(BF16), wb_ref[...], preferred_element_type=F32)
    o_ref[...] = x_ref[...] + gt_ref[...] * acc.reshape(o_ref.shape)


def _out_projection(x3, mix_a, mix_b, col_a, col_b, layer, w_out_bf, mod3, bb, rb):
    nb, r, d = x3.shape
    tm = bb * rb
    tn = d
    rblocks = r // rb
    n_m = (nb // bb) * rblocks
    kh = QA_W
    blocks = 2 * (2 * _nbytes((tm, tn), F32) + _nbytes((tm, kh), mix_a.dtype) + _nbytes((tm, kh), mix_b.dtype)
                  + 2 * _nbytes((kh, tn), BF16) + _nbytes((bb, SUBLANES, tn), F32))
    return pl.pallas_call(
        _outproj_kernel,
        grid=(n_m, d // tn),
        in_specs=[pl.BlockSpec((bb, rb, tn), lambda i, j: (i // rblocks, i % rblocks, j)),
                  pl.BlockSpec((tm, kh), lambda i, j: (i, col_a)),
                  pl.BlockSpec((tm, kh), lambda i, j: (i, col_b)),
                  pl.BlockSpec((None, kh, tn), lambda i, j: (layer, 0, j)),
                  pl.BlockSpec((None, kh, tn), lambda i, j: (layer, 1, j)),
                  pl.BlockSpec((bb, 1, tn), lambda i, j: (i // rblocks, 0, 2 * (d // tn) + j))],
        out_specs=pl.BlockSpec((bb, rb, tn), lambda i, j: (i // rblocks, i % rblocks, j)),
        out_shape=jax.ShapeDtypeStruct(x3.shape, F32),
        compiler_params=pltpu.CompilerParams(dimension_semantics=("parallel", "parallel"),
                                             vmem_limit_bytes=_vmem_limit(blocks)),
        name="out_projection",
    )(x3, mix_a, mix_b, w_out_bf, w_out_bf, mod3)


def _ffn_kernel(x_ref, sc_ref, sh_ref, gt_ref, g_ref, wg_ref, wu_ref, wd_ref, wc_ref, bc_ref, hist_ref,
                o_ref, cs_ref, h_scr, gate_scr, up_scr, act_scr, carry_scr, *, rblocks, chunk_seqs, chunk_rows):
    i = pl.program_id(0)
    f = pl.program_id(1)
    n_f = pl.num_programs(1)
    bb, rb, d = x_ref.shape
    tm, tf = bb * rb, wg_ref.shape[1]

    @pl.when(f == 0)
    def _():
        _norm_modulate_rows(x_ref, g_ref, sc_ref, sh_ref, h_scr)
        o_ref[...] = jnp.zeros(o_ref.shape, F32)

    if rblocks > 1:
        @pl.when(i % rblocks == 0)
        def _():
            carry_scr[f] = hist_ref[...]
        history = lambda s0, n: carry_scr[f]
    else:
        history = lambda s0, n: hist_ref[s0:s0 + n]

    h = h_scr[...]
    gate_scr[0:SUBLANES, :] = jnp.zeros((SUBLANES, tf), F32)
    gate_scr[SUBLANES:, :] = jnp.dot(h, wg_ref[...], preferred_element_type=F32)
    up_scr[...] = jnp.dot(h, wu_ref[...], preferred_element_type=F32)

    n_rows = chunk_seqs * chunk_rows
    shape = (chunk_seqs, chunk_rows, tf)
    t = lax.broadcasted_iota(jnp.int32, shape, 1)
    for r0 in range(0, tm, n_rows):
        def shifted(s):
            lo = SUBLANES + r0 - s
            return gate_scr[lo:lo + n_rows, :].reshape(shape)
        g0, g1, g2 = shifted(0), shifted(1), shifted(2)
        if r0 % rb == 0:
            hist = history(r0 // rb, chunk_seqs)
            g1 = jnp.where(t >= 1, g1, hist[:, 1:2, :])
            g2 = jnp.where(t >= 2, g2, jnp.where(t == 0, hist[:, 0:1, :], hist[:, 1:2, :]))
        gc = bc_ref[...] + wc_ref[0:1, :] * g2 + wc_ref[1:2, :] * g1 + wc_ref[2:3, :] * g0
        act = jax.nn.gelu(gc, approximate=True) * up_scr[r0:r0 + n_rows, :].reshape(shape)
        act_scr[r0:r0 + n_rows, :] = act.reshape(n_rows, tf).astype(BF16)
        if (r0 + n_rows) % rb == 0:
            s0 = (r0 + n_rows) // rb - chunk_seqs
            tail = g0[:, chunk_rows - (CONV_W - 1):, :]
            cs_ref[s0:s0 + chunk_seqs] = tail
            if rblocks > 1:
                carry_scr[f] = tail
    o_ref[...] += jnp.dot(act_scr[...], wd_ref[...], preferred_element_type=F32).reshape(bb, rb, d)

    @pl.when(f == n_f - 1)
    def _():
        o_ref[...] = x_ref[...] + gt_ref[...] * o_ref[...]


FFN_CHUNK_ROWS = 64


def _conv_ffn(x3, mod3, layer, g_ffn, wg_bf, wu_bf, wd_bf, w_conv, b_conv, hist, bb, rb):
    nb, r, d = x3.shape
    f_pad = wg_bf.shape[2]
    tf = FF_TILE
    n_f = f_pad // tf
    tm = bb * rb
    rblocks = r // rb
    assert rblocks == 1 or bb == 1
    n_m = (nb // bb) * rblocks
    carry_slots = n_f if rblocks > 1 else 1
    chunk_rows = min(rb, FFN_CHUNK_ROWS)
    chunk_seqs = FFN_CHUNK_ROWS // chunk_rows
    assert rb % chunk_rows == 0 and bb % chunk_seqs == 0 and chunk_rows >= CONV_W - 1
    modmap = lambda k: (lambda i, f: (i // rblocks, 0, k))
    blocks = (2 * (2 * _nbytes((tm, d), F32) + 2 * _nbytes((d, tf), BF16) + _nbytes((tf, d), BF16)
                   + 2 * _nbytes((bb, SUBLANES, tf), F32) + 3 * _nbytes((bb, SUBLANES, d), F32))
              + _nbytes((tm, d), BF16) + carry_slots * _nbytes((bb, SUBLANES, tf), F32)
              + 3 * _nbytes((tm + SUBLANES, tf), F32))
    y, tails = pl.pallas_call(
        functools.partial(_ffn_kernel, rblocks=rblocks, chunk_seqs=chunk_seqs, chunk_rows=chunk_rows),
        grid=(n_m, n_f),
        in_specs=[pl.BlockSpec((bb, rb, d), lambda i, f: (i // rblocks, i % rblocks, 0)),
                  pl.BlockSpec((bb, 1, d), modmap(4)),
                  pl.BlockSpec((bb, 1, d), modmap(3)),
                  pl.BlockSpec((bb, 1, d), modmap(5)),
                  pl.BlockSpec((1, d), lambda i, f: (0, 0)),
                  pl.BlockSpec((None, d, tf), lambda i, f: (layer, 0, f)),
                  pl.BlockSpec((None, d, tf), lambda i, f: (layer, 0, f)),
                  pl.BlockSpec((None, tf, d), lambda i, f: (layer, f, 0)),
                  pl.BlockSpec((None, CONV_W, tf), lambda i, f: (layer, 0, f)),
                  pl.BlockSpec((None, 1, tf), lambda i, f: (layer, 0, f)),
                  pl.BlockSpec((bb, CONV_W - 1, tf), lambda i, f: (i // rblocks, 0, f))],
        out_specs=[pl.BlockSpec((bb, rb, d), lambda i, f: (i // rblocks, i % rblocks, 0)),
                   pl.BlockSpec((bb, CONV_W - 1, tf), lambda i, f: (i, 0, f))],
        out_shape=[jax.ShapeDtypeStruct(x3.shape, F32),
                   jax.ShapeDtypeStruct((n_m * bb, CONV_W - 1, f_pad), F32)],
        scratch_shapes=[pltpu.VMEM((tm, d), BF16),
                        pltpu.VMEM((tm + SUBLANES, tf), F32),
                        pltpu.VMEM((tm, tf), F32),
                        pltpu.VMEM((tm, tf), BF16),
                        pltpu.VMEM((carry_slots, bb, CONV_W - 1, tf), F32)],
        compiler_params=pltpu.CompilerParams(dimension_semantics=("arbitrary", "arbitrary"),
                                             vmem_limit_bytes=_vmem_limit(blocks)),
        name="conv_ffn",
    )(x3, mod3, mod3, mod3, g_ffn.reshape(1, d), wg_bf, wu_bf, wd_bf, w_conv, b_conv, hist)
    state = tails.reshape(nb // bb, rblocks, bb, CONV_W - 1, f_pad)[:, rblocks - 1]
    return y, state.reshape(nb, CONV_W - 1, f_pad)


def _pad_last(a, n):
    return jnp.pad(a, [(0, 0)] * (a.ndim - 1) + [(0, n - a.shape[-1])])


def _cast_pad_cols_kernel(x_ref, o_ref):
    n = x_ref.shape[1]
    o_ref[:, :n] = x_ref[...].astype(BF16)
    o_ref[:, n:] = jnp.zeros((o_ref.shape[0], o_ref.shape[1] - n), BF16)


def _cast_pad_cols(w, n_pad):
    depth, rows, n = w.shape
    tr = 256
    blocks = 2 * (_nbytes((tr, n), F32) + _nbytes((tr, n_pad), BF16))
    return pl.pallas_call(
        _cast_pad_cols_kernel,
        grid=(depth, rows // tr),
        in_specs=[pl.BlockSpec((None, tr, n), lambda l, i: (l, i, 0))],
        out_specs=pl.BlockSpec((None, tr, n_pad), lambda l, i: (l, i, 0)),
        out_shape=jax.ShapeDtypeStruct((depth, rows, n_pad), BF16),
        compiler_params=pltpu.CompilerParams(dimension_semantics=("parallel", "parallel"),
                                             vmem_limit_bytes=_vmem_limit(blocks)),
        name="cast_pad_cols",
    )(w)


def kernel(x_prompt, x_sample, c_prompt, c_sample, cache_a_k, cache_a_v, cache_b_k, cache_b_v, state_conv, rel_bias, w_ada, b_ada, g_attn, g_ffn, w_in, g_qn_a, g_kn_a, g_qn_b, g_kn_b, sinks, w_out, w_gate, w_up, w_conv, b_conv, w_down):
    depth = w_in.shape[0]
    batch, seq, d = x_prompt.shape
    db, s_new, _ = x_sample.shape
    d_ff = w_gate.shape[2]
    f_pad = -(-d_ff // FF_TILE) * FF_TILE
    la, lb = cache_a_k.shape[2], cache_b_k.shape[2]
    assert batch == 1 and seq % CHUNK_A == 0 and s_new == SUBLANES

    pb = _bias_by_distance(rel_bias, max(WIN_A + 1, la + s_new)) * LOG2E
    bias_a, bias_b = _prompt_bias_tables(pb)
    mb_a, mb_b = _sample_tables(pb, s_new, la, lb)
    sinks2 = sinks.astype(F32) * LOG2E

    c_all = jnp.concatenate([c_prompt, jnp.zeros((SUBLANES - batch, d), F32), c_sample], axis=0)

    ones = jnp.ones((HEAD_DIM,), F32)
    flag = jnp.concatenate([jnp.ones((2 * QA_W,), F32), jnp.zeros((QA_W,), F32), jnp.ones((QB_W + KB_W,), F32),
                            jnp.zeros((KB_W,), F32)]).reshape(1, IN_W)

    w_in_bf = w_in.astype(BF16)
    w_out_bf = w_out.astype(BF16)
    wg_bf = _cast_pad_cols(w_gate, f_pad)
    wu_bf = _cast_pad_cols(w_up, f_pad)
    wd_bf = jnp.pad(w_down.astype(BF16), ((0, 0), (0, f_pad - d_ff), (0, 0)))
    wc = _pad_last(w_conv, f_pad)
    bc = _pad_last(b_conv, f_pad).reshape(depth, 1, f_pad)
    mod_all = _modulation(c_all, w_ada, b_ada)

    xp = x_prompt
    xs = x_sample
    hist_p = jnp.zeros((batch, CONV_W - 1, f_pad), F32)
    outs = [[] for _ in range(10)]
    for l in range(depth):
        gain = jnp.concatenate([jnp.tile(g_qn_a[l], N_HEADS_A), jnp.tile(g_kn_a[l], N_HEADS_A),
                                jnp.tile(ones, N_HEADS_A), jnp.tile(g_qn_b[l], N_HEADS_B),
                                jnp.tile(g_kn_b[l], N_KV_B), jnp.tile(ones, N_KV_B)]).reshape(1, IN_W)
        sink_rows = jnp.broadcast_to(jnp.repeat(sinks2[l], s_new)[:, None], (N_HEADS_B * s_new, LANES))
        mod_p = mod_all[l, 0:batch].reshape(batch, 1, 6 * d)
        mod_s = mod_all[l, SUBLANES:].reshape(db, 1, 6 * d)

        qkv_p = _projection(xp, mod_p, l, g_attn[l], w_in_bf, gain, flag, 1, 1024)
        mix_pa = _attn_a_prompt(qkv_p, bias_a)
        mix_pb = _attn_b_prompt(qkv_p, bias_b, sinks2[l])
        xp = _out_projection(xp, mix_pa, mix_pb, 0, 0, l, w_out_bf, mod_p, 1, 512)
        xp, conv_p = _conv_ffn(xp, mod_p, l, g_ffn[l], wg_bf, wu_bf, wd_bf, wc, bc, hist_p, 1, 1024)
        keep_a, keep_b = min(WIN_A, seq), min(WIN_B, seq)
        outs[0].append(qkv_p[seq - keep_a:, COL_KA * LANES:COL_KA * LANES + QA_W].reshape(batch, keep_a, N_HEADS_A, HEAD_DIM))
        outs[1].append(qkv_p[seq - keep_a:, COL_VA * LANES:COL_VA * LANES + QA_W].reshape(batch, keep_a, N_HEADS_A, HEAD_DIM))
        outs[2].append(qkv_p[seq - keep_b:, COL_KB * LANES:COL_KB * LANES + KB_W].reshape(batch, keep_b, N_KV_B, HEAD_DIM))
        outs[3].append(qkv_p[seq - keep_b:, COL_VB * LANES:COL_VB * LANES + KB_W].reshape(batch, keep_b, N_KV_B, HEAD_DIM))
        outs[4].append(conv_p[:, :, :d_ff])

        qkv_s = _projection(xs, mod_s, l, g_attn[l], w_in_bf, gain, flag, db, s_new)
        mix_s = _attn_sample(qkv_s, l, cache_a_k, cache_a_v, cache_b_k, cache_b_v,
                             mb_a, mb_b, sink_rows, s_new)
        xs = _out_projection(xs, mix_s, mix_s, 0, 1, l, w_out_bf, mod_s, db // 2, s_new)
        xs, conv_s = _conv_ffn(xs, mod_s, l, g_ffn[l], wg_bf, wu_bf, wd_bf, wc, bc,
                               _pad_last(state_conv[l], f_pad), db // 2, s_new)
        outs[5].append(qkv_s[:, COL_KA * LANES:COL_KA * LANES + QA_W].reshape(db, s_new, N_HEADS_A, HEAD_DIM))
        outs[6].append(qkv_s[:, COL_VA * LANES:COL_VA * LANES + QA_W].reshape(db, s_new, N_HEADS_A, HEAD_DIM))
        outs[7].append(qkv_s[:, COL_KB * LANES:COL_KB * LANES + KB_W].reshape(db, s_new, N_KV_B, HEAD_DIM))
        outs[8].append(qkv_s[:, COL_VB * LANES:COL_VB * LANES + KB_W].reshape(db, s_new, N_KV_B, HEAD_DIM))
        outs[9].append(conv_s[:, :, :d_ff])

    return (xp, xs) + tuple(jnp.stack(o) for o in outs)
```

```python
import functools
import math

import numpy as np
import jax
import jax.numpy as jnp
from jax import lax
from jax.experimental import pallas as pl
from jax.experimental.pallas import tpu as pltpu

F32 = jnp.float32
BF16 = jnp.bfloat16

LANES = 128
SUBLANES = 8
VMEM_BYTES_V7X = 64 * 1024 * 1024

HEAD_DIM = 128
N_HEADS_A = 8
N_HEADS_B = 8
N_KV_B = 2
GROUP_B = N_HEADS_B // N_KV_B
A_BRANCHES = ((128, 1), (512, 4), (2048, 16))
WIN_A = 2048
WIN_B = 128
BLOCK = 128
N_BUCKETS = 32
MAX_DISTANCE = 2048
CONV_W = 3
EPS = 1e-6
NEG = -1e30
SCALE = HEAD_DIM ** -0.5
LOG2E = math.log2(math.e)
QK_SCALE_LOG2 = SCALE * LOG2E

QA_W = N_HEADS_A * HEAD_DIM
QB_W = N_HEADS_B * HEAD_DIM
KB_W = N_KV_B * HEAD_DIM
IN_W = 3 * QA_W + QB_W + 2 * KB_W
COL_QA, COL_KA, COL_VA = 0, QA_W // LANES, 2 * QA_W // LANES
COL_QB = 3 * QA_W // LANES
COL_KB = COL_QB + QB_W // LANES
COL_VB = COL_KB + KB_W // LANES

CHUNK_A = WIN_A
CHUNK_B = 1024
FF_TILE = 512


def _vmem_limit(block_bytes):
    return int(min(VMEM_BYTES_V7X - (4 << 20), block_bytes + (12 << 20)))


def _nbytes(shape, dtype):
    return int(np.prod(shape)) * jnp.dtype(dtype).itemsize


def _t5_bucket_np(dist):
    dist = np.maximum(dist, 0)
    max_exact = N_BUCKETS // 2
    ratio = np.log(np.maximum(dist, 1).astype(np.float64) / max_exact) / math.log(MAX_DISTANCE / max_exact)
    large = max_exact + (ratio * (N_BUCKETS - max_exact)).astype(np.int32)
    large = np.minimum(large, N_BUCKETS - 1)
    return np.where(dist < max_exact, dist, large).astype(np.int32)


def _bias_by_distance_rev(rel_bias, n):
    buckets = _t5_bucket_np(np.arange(n))
    assert np.all(np.diff(buckets) >= 0)
    runs = np.bincount(buckets, minlength=N_BUCKETS)
    tab = rel_bias.astype(F32).T
    return jnp.concatenate([jnp.broadcast_to(tab[:, k:k + 1], (tab.shape[0], int(runs[k])))
                            for k in reversed(range(N_BUCKETS)) if runs[k] > 0], axis=1)


def _banded_table(vals_rev):
    hh, n = vals_rev.shape
    period = 4 * BLOCK
    w = jnp.concatenate([jnp.full((hh, BLOCK + 1 - n), NEG, F32), vals_rev,
                         jnp.full((hh, period - (BLOCK + 1)), NEG, F32)], axis=1)
    skew = jnp.tile(w, (1, BLOCK))[:, :BLOCK * (period - 1)].reshape(hh, BLOCK, period - 1)
    return skew[:, :, :2 * BLOCK]


def _prompt_bias_tables(pb_rev):
    n = pb_rev.shape[1]
    bias_a = jnp.stack([_banded_table(pb_rev[:N_HEADS_A, n - 1 - BLOCK * dil::dil]) for _, dil in A_BRANCHES])
    bias_b = _banded_table(pb_rev[N_HEADS_A:, n - WIN_B:])
    return bias_a, bias_b.reshape(N_KV_B, GROUP_B, BLOCK, 2 * BLOCK)


DIL_MAX = A_BRANCHES[-1][1]
DIL_MID = A_BRANCHES[1][1]
assert A_BRANCHES[0][1] == 1 and DIL_MAX == DIL_MID ** 2
SA_Y_POS = A_BRANCHES[1][0]
SA_X_M = (WIN_A - SA_Y_POS) // DIL_MAX
SA_X_SPLIT = 3
SA_X_COLS = SA_X_M * SUBLANES * N_HEADS_A
SA_Y_COLS = SA_Y_POS * N_HEADS_A
SA_CHUNK = 768
SA_NEW_COLS = -(SA_X_COLS + SA_Y_COLS + N_HEADS_A * SUBLANES) % SA_CHUNK + N_HEADS_A * SUBLANES
SA_COLS = SA_X_COLS + SA_Y_COLS + SA_NEW_COLS
assert SA_COLS % SA_CHUNK == 0 and SA_NEW_COLS % (2 * SUBLANES) == 0
SB_CACHE_COLS = WIN_B * N_KV_B
SB_NEW_COLS = LANES
SB_COLS = SB_CACHE_COLS + SB_NEW_COLS


def _sample_columns(s_new, la, lb):
    m, res, h = np.meshgrid(np.arange(SA_X_M), np.arange(SUBLANES), np.arange(N_HEADS_A), indexing="ij")
    pos_x, head_x = (DIL_MAX * m + res).reshape(-1), h.reshape(-1)
    p, h = np.meshgrid(np.arange(la - SA_Y_POS, la), np.arange(N_HEADS_A), indexing="ij")
    pos_y, head_y = p.reshape(-1), h.reshape(-1)
    h, t = np.meshgrid(np.arange(N_HEADS_A), np.arange(s_new), indexing="ij")
    pad = np.full((SA_NEW_COLS - N_HEADS_A * s_new,), -1)
    pos_a = np.concatenate([pos_x, pos_y, la + t.reshape(-1), pad])
    head_a = np.concatenate([head_x, head_y, h.reshape(-1), pad])
    p, g = np.meshgrid(np.arange(lb), np.arange(N_KV_B), indexing="ij")
    g2, t = np.meshgrid(np.arange(N_KV_B), np.arange(s_new), indexing="ij")
    pad = np.full((SB_NEW_COLS - N_KV_B * s_new,), -1)
    pos_b = np.concatenate([p.reshape(-1), lb + t.reshape(-1), pad])
    head_b = np.concatenate([g.reshape(-1), g2.reshape(-1), pad])
    return pos_a, head_a, pos_b, head_b


def _query_key_bias_t(pbh_rev, length, s_new):
    n = length + s_new
    rev = jnp.pad(pbh_rev[:, pbh_rev.shape[1] - n:], ((0, 0), (0, s_new - 1)))
    a = jnp.stack([rev[:, s_new - 1 - i:s_new - 1 - i + n] for i in range(s_new)], axis=1)
    return a.reshape(-1, n).T


def _key_rows(at, pos_major, n_key_heads):
    return jnp.repeat(at, n_key_heads, axis=0) if pos_major else jnp.tile(at, (n_key_heads, 1))


def _sample_tables(pb_rev, s_new, la, lb):
    assert la == WIN_A and lb == WIN_B and s_new == SUBLANES
    pos_a, head_a, pos_b, head_b = _sample_columns(s_new, la, lb)
    qh, qi = np.meshgrid(np.arange(N_HEADS_A), np.arange(s_new), indexing="ij")
    qh, qi = qh.reshape(-1, 1), qi.reshape(-1, 1)

    dist = la + qi - pos_a[None, :]
    real = (pos_a >= 0)[None, :] & (dist >= 0) & (head_a[None, :] == qh)
    count = np.zeros(dist.shape, np.float32)
    for win, dil in A_BRANCHES:
        count += (real & (dist % dil == 0) & (dist <= win)).astype(np.float32)
    n_q = N_HEADS_A * s_new
    at = _query_key_bias_t(pb_rev[:N_HEADS_A], la, s_new)
    at_x = at[:la - SA_Y_POS].reshape(SA_X_M, DIL_MAX, n_q)[:, :SUBLANES].reshape(-1, n_q)
    rows_a = jnp.concatenate([
        _key_rows(at_x, True, N_HEADS_A),
        _key_rows(at[la - SA_Y_POS:la], True, N_HEADS_A),
        _key_rows(at[la:], False, N_HEADS_A),
        jnp.zeros((SA_NEW_COLS - N_HEADS_A * s_new, n_q), F32)], axis=0)
    log2_count = np.log2(np.maximum(count, 1.0)).T
    mb_a = jnp.where(jnp.asarray(count.T > 0), rows_a + jnp.asarray(log2_count), NEG).T

    dist_b = lb + qi - pos_b[None, :]
    valid_b = (pos_b >= 0)[None, :] & (dist_b >= 0) & (dist_b < WIN_B) & (head_b[None, :] == qh // GROUP_B)
    bt = _query_key_bias_t(pb_rev[N_HEADS_A:], lb, s_new)
    rows_b = jnp.concatenate([
        _key_rows(bt[:lb], True, N_KV_B),
        _key_rows(bt[lb:], False, N_KV_B),
        jnp.zeros((SB_NEW_COLS - N_KV_B * s_new, n_q), F32)], axis=0)
    mb_b = jnp.where(jnp.asarray(valid_b.T), rows_b, NEG).T
    return mb_a, mb_b


def _split_bf16(a):
    hi = a.astype(BF16)
    lo = (a - hi.astype(F32)).astype(BF16)
    return hi, lo


def _mod_kernel(c_ref, w_ref, b_ref, o_ref):
    c = c_ref[...]
    s_hi, s_lo = _split_bf16(c * jax.nn.sigmoid(c))
    w_hi, w_lo = _split_bf16(w_ref[...])
    acc = jnp.dot(s_hi, w_hi, preferred_element_type=F32)
    acc += jnp.dot(s_hi, w_lo, preferred_element_type=F32)
    acc += jnp.dot(s_lo, w_hi, preferred_element_type=F32)
    o_ref[...] = acc + b_ref[...]


def _modulation(c_all, w_ada, b_ada):
    m, d = c_all.shape
    depth, _, n = w_ada.shape
    tn = 1024
    blocks = 2 * (_nbytes((d, tn), F32) + _nbytes((m, tn), F32)) + _nbytes((m, d), F32) * 2 + 2 * _nbytes((d, tn), BF16)
    return pl.pallas_call(
        _mod_kernel,
        grid=(depth, n // tn),
        in_specs=[pl.BlockSpec((m, d), lambda l, j: (0, 0)),
                  pl.BlockSpec((None, d, tn), lambda l, j: (l, 0, j)),
                  pl.BlockSpec((None, 1, tn), lambda l, j: (l, 0, j))],
        out_specs=pl.BlockSpec((None, m, tn), lambda l, j: (l, 0, j)),
        out_shape=jax.ShapeDtypeStruct((depth, m, n), F32),
        compiler_params=pltpu.CompilerParams(dimension_semantics=("parallel", "parallel"),
                                             vmem_limit_bytes=_vmem_limit(blocks)),
        name="modulation",
    )(c_all, w_ada, b_ada.reshape(depth, 1, n))


def _norm_modulate(x, g, sc, sh):
    ms = jnp.mean(x * x, axis=-1, keepdims=True)
    return (x * lax.rsqrt(ms + EPS) * g) * (1.0 + sc) + sh


NORM_ROWS = 16


def _norm_modulate_rows(x_ref, g_ref, sc_ref, sh_ref, h_scr):
    bb, rb, d = x_ref.shape
    cr = min(rb, NORM_ROWS)
    cs = NORM_ROWS // cr
    for r0 in range(0, bb * rb, NORM_ROWS):
        s, t0 = r0 // rb, r0 % rb
        h = _norm_modulate(x_ref[s:s + cs, t0:t0 + cr, :], g_ref[...], sc_ref[s:s + cs], sh_ref[s:s + cs])
        h_scr[r0:r0 + NORM_ROWS, :] = h.reshape(NORM_ROWS, d).astype(BF16)


def _dot_nt(a, b):
    return lax.dot_general(a, b, (((1,), (1,)), ((), ())), preferred_element_type=F32)


def _proj_kernel(x_ref, sc_ref, sh_ref, g_ref, w_ref, gain_ref, flag_ref, o_ref, h_scr, acc_scr):
    @pl.when(pl.program_id(1) == 0)
    def _():
        _norm_modulate_rows(x_ref, g_ref, sc_ref, sh_ref, h_scr)

    acc_scr[...] = jnp.dot(h_scr[...], w_ref[...], preferred_element_type=F32)
    tm, tn = acc_scr.shape
    for r0 in range(0, tm, 64):
        for c in range(tn // LANES):
            sl = slice(c * LANES, (c + 1) * LANES)
            blk = acc_scr[r0:r0 + 64, sl]
            ms = jnp.mean(blk * blk, axis=-1, keepdims=True)
            nrm = blk * lax.rsqrt(ms + EPS) * gain_ref[:, sl]
            o_ref[r0:r0 + 64, sl] = jnp.where(flag_ref[:, sl] > 0.0, nrm, blk)


def _projection(x3, mod3, layer, g_attn, w_in_bf, gain, flag, bb, rb):
    nb, r, d = x3.shape
    tm = bb * rb
    tn = 768
    n_m = (nb // bb) * (r // rb)
    rblocks = r // rb
    xmap = lambda i, j: (i // rblocks, i % rblocks, 0)
    blocks = (2 * (_nbytes((tm, d), F32) + _nbytes((d, tn), BF16) + _nbytes((tm, tn), F32))
              + _nbytes((tm, d), BF16) + 8 * _nbytes((bb, SUBLANES, d), F32) + _nbytes((tm, tn), F32))
    return pl.pallas_call(
        _proj_kernel,
        grid=(n_m, IN_W // tn),
        in_specs=[pl.BlockSpec((bb, rb, d), xmap),
                  pl.BlockSpec((bb, 1, d), lambda i, j: (i // rblocks, 0, 1)),
                  pl.BlockSpec((bb, 1, d), lambda i, j: (i // rblocks, 0, 0)),
                  pl.BlockSpec((1, d), lambda i, j: (0, 0)),
                  pl.BlockSpec((None, d, tn), lambda i, j: (layer, 0, j)),
                  pl.BlockSpec((1, tn), lambda i, j: (0, j)),
                  pl.BlockSpec((1, tn), lambda i, j: (0, j))],
        out_specs=pl.BlockSpec((tm, tn), lambda i, j: (i, j)),
        out_shape=jax.ShapeDtypeStruct((nb * r, IN_W), F32),
        scratch_shapes=[pltpu.VMEM((tm, d), BF16), pltpu.VMEM((tm, tn), F32)],
        compiler_params=pltpu.CompilerParams(dimension_semantics=("parallel", "arbitrary"),
                                             vmem_limit_bytes=_vmem_limit(blocks)),
        name="projection",
    )(x3, mod3, mod3, g_attn.reshape(1, d), w_in_bf, gain, flag)


def _attn_a_prompt_kernel(q_ref, kc_ref, kp_ref, vc_ref, vp_ref, bias_ref, o_ref, num_scr, m_scr, s_scr, d4_scr):
    first_chunk = pl.program_id(0) == 0
    chunk = q_ref.shape[0]
    plane = chunk // DIL_MID
    srcs = (q_ref, kc_ref, kp_ref, vc_ref, vp_ref)
    for a, ref in enumerate(srcs):
        for p in range(DIL_MID):
            d4_scr[a, p] = ref[pl.ds(p, plane, stride=DIL_MID), :]

    def plane_rows(r, ub, dil):
        step = dil // DIL_MID
        lo = r // DIL_MID + step * BLOCK * ub
        return r % DIL_MID, (slice(lo, lo + BLOCK) if step == 1 else pl.ds(lo, BLOCK, stride=step))

    def tile(a, r, ub, dil):
        if dil == 1:
            return srcs[a][ub * BLOCK:(ub + 1) * BLOCK, :]
        p, rows = plane_rows(r, ub, dil)
        return d4_scr[a, p, rows, :]

    col = lax.broadcasted_iota(jnp.int32, (BLOCK, 2 * BLOCK), 1)
    ones = jnp.ones((2 * BLOCK, LANES), BF16)
    for bi, (_, dil) in enumerate(A_BRANCHES):
        nsub = chunk // (BLOCK * dil)
        bias = bias_ref[bi]
        for r in range(dil):
            k_prev = tile(2, r, nsub - 1, dil).astype(BF16)
            v_prev = tile(4, r, nsub - 1, dil).astype(BF16)
            for ub in range(nsub):
                q = (tile(0, r, ub, dil) * QK_SCALE_LOG2).astype(BF16)
                k_cur = tile(1, r, ub, dil).astype(BF16)
                v_cur = tile(3, r, ub, dil).astype(BF16)
                z = _dot_nt(q, jnp.concatenate([k_prev, k_cur], axis=0)) + bias
                if ub == 0:
                    z = jnp.where(jnp.logical_and(first_chunk, col < BLOCK), NEG, z)
                m = jnp.max(z, axis=-1, keepdims=True)
                e = jnp.exp2(z - m).astype(BF16)
                v_ext = jnp.concatenate([jnp.concatenate([v_prev, v_cur], axis=0), ones], axis=1)
                num = jnp.dot(e, v_ext, preferred_element_type=F32)
                if dil == 1:
                    rows = slice(ub * BLOCK, (ub + 1) * BLOCK)
                else:
                    p, rows = plane_rows(r, ub, dil)
                    rows = (slice(p * plane + rows.start, p * plane + rows.stop) if isinstance(rows, slice)
                            else pl.ds(p * plane + rows.start, BLOCK, stride=rows.stride))
                num_scr[bi, rows, :] = num[:, :HEAD_DIM]
                m_scr[bi, rows, :] = jnp.broadcast_to(m, (BLOCK, LANES))
                s_scr[bi, rows, :] = num[:, HEAD_DIM:]
                k_prev, v_prev = k_cur, v_cur
    quarter = plane // DIL_MID
    for p in range(DIL_MID):
        def stat(scr, bi):
            if bi == 0:
                return scr[0, pl.ds(p, plane, stride=DIL_MID), :]
            return scr[bi, p * plane:(p + 1) * plane, :]
        ms = [stat(m_scr, bi) for bi in range(len(A_BRANCHES))]
        m_all = functools.reduce(jnp.maximum, ms)
        top = jnp.zeros_like(m_all)
        bot = jnp.zeros_like(m_all)
        for bi in range(len(A_BRANCHES)):
            w = jnp.exp2(ms[bi] - m_all)
            top += w * stat(num_scr, bi)
            bot += w * stat(s_scr, bi)
        out = top / bot
        for t4 in range(DIL_MID):
            d4_scr[0, t4, pl.ds(p, quarter, stride=DIL_MID), :] = out[t4 * quarter:(t4 + 1) * quarter]
    for t4 in range(DIL_MID):
        o_ref[t4 * plane:(t4 + 1) * plane, :] = d4_scr[0, t4].astype(o_ref.dtype)


def _attn_a_prompt(qkv, bias_a):
    t = qkv.shape[0]
    n_chunks = t // CHUNK_A
    blk = (CHUNK_A, HEAD_DIM)
    prev = lambda b: jnp.maximum(b - 1, 0)
    blocks = (2 * (5 * _nbytes(blk, F32) + _nbytes((3, BLOCK, 2 * BLOCK), F32) + _nbytes(blk, BF16))
              + 14 * _nbytes(blk, F32))
    return pl.pallas_call(
        _attn_a_prompt_kernel,
        grid=(n_chunks, N_HEADS_A),
        in_specs=[pl.BlockSpec(blk, lambda b, h: (b, COL_QA + h)),
                  pl.BlockSpec(blk, lambda b, h: (b, COL_KA + h)),
                  pl.BlockSpec(blk, lambda b, h: (prev(b), COL_KA + h)),
                  pl.BlockSpec(blk, lambda b, h: (b, COL_VA + h)),
                  pl.BlockSpec(blk, lambda b, h: (prev(b), COL_VA + h)),
                  pl.BlockSpec((3, None, BLOCK, 2 * BLOCK), lambda b, h: (0, h, 0, 0))],
        out_specs=pl.BlockSpec(blk, lambda b, h: (b, h)),
        out_shape=jax.ShapeDtypeStruct((t, QA_W), BF16),
        scratch_shapes=[pltpu.VMEM((3, CHUNK_A, HEAD_DIM), F32)] * 3
        + [pltpu.VMEM((5, DIL_MID, CHUNK_A // DIL_MID, HEAD_DIM), F32)],
        compiler_params=pltpu.CompilerParams(dimension_semantics=("parallel", "parallel"),
                                             vmem_limit_bytes=_vmem_limit(blocks)),
        name="mixer_a_prompt",
    )(qkv, qkv, qkv, qkv, qkv, bias_a)


def _attn_b_prompt_kernel(sink_ref, q_ref, kc_ref, kp_ref, vc_ref, vp_ref, bias_ref, o_ref):
    first_chunk = pl.program_id(0) == 0
    g = pl.program_id(1)
    col = lax.broadcasted_iota(jnp.int32, (BLOCK, 2 * BLOCK), 1)
    k_prev = kp_ref[...].astype(BF16)
    v_prev = vp_ref[...].astype(BF16)
    for blk in range(q_ref.shape[0] // BLOCK):
        rows = slice(blk * BLOCK, (blk + 1) * BLOCK)
        k_cur = kc_ref[rows, :].astype(BF16)
        v_cur = vc_ref[rows, :].astype(BF16)
        kcat = jnp.concatenate([k_prev, k_cur], axis=0)
        vcat = jnp.concatenate([v_prev, v_cur], axis=0)
        for j in range(GROUP_B):
            cols = slice(j * HEAD_DIM, (j + 1) * HEAD_DIM)
            sink = sink_ref[g * GROUP_B + j]
            z = _dot_nt((q_ref[rows, cols] * QK_SCALE_LOG2).astype(BF16), kcat) + bias_ref[j]
            if blk == 0:
                z = jnp.where(jnp.logical_and(first_chunk, col < BLOCK), NEG, z)
            m = jnp.maximum(jnp.max(z, axis=-1, keepdims=True), sink)
            e = jnp.exp2(z - m)
            denom = jnp.sum(e, axis=-1, keepdims=True) + jnp.exp2(sink - m)
            num = jnp.dot(e.astype(BF16), vcat, preferred_element_type=F32)
            o_ref[rows, cols] = (num / denom).astype(o_ref.dtype)
        k_prev, v_prev = k_cur, v_cur


def _attn_b_prompt(qkv, bias_b, sinks):
    t = qkv.shape[0]
    n_chunks = t // CHUNK_B
    per = CHUNK_B // BLOCK
    qblk = (CHUNK_B, GROUP_B * HEAD_DIM)
    kblk = (CHUNK_B, HEAD_DIM)
    pblk = (BLOCK, HEAD_DIM)
    qcol = COL_QB // GROUP_B
    prev = lambda b: jnp.maximum(b * per - 1, 0)
    blocks = 2 * (_nbytes(qblk, F32) + 2 * _nbytes(kblk, F32) + 2 * _nbytes(pblk, F32)
                  + _nbytes((GROUP_B, BLOCK, 2 * BLOCK), F32) + _nbytes(qblk, BF16))
    return pl.pallas_call(
        _attn_b_prompt_kernel,
        grid=(n_chunks, N_KV_B),
        in_specs=[pl.BlockSpec(memory_space=pltpu.SMEM),
                  pl.BlockSpec(qblk, lambda b, g: (b, qcol + g)),
                  pl.BlockSpec(kblk, lambda b, g: (b, COL_KB + g)),
                  pl.BlockSpec(pblk, lambda b, g: (prev(b), COL_KB + g)),
                  pl.BlockSpec(kblk, lambda b, g: (b, COL_VB + g)),
                  pl.BlockSpec(pblk, lambda b, g: (prev(b), COL_VB + g)),
                  pl.BlockSpec((None, GROUP_B, BLOCK, 2 * BLOCK), lambda b, g: (g, 0, 0, 0))],
        out_specs=pl.BlockSpec(qblk, lambda b, g: (b, g)),
        out_shape=jax.ShapeDtypeStruct((t, QB_W), BF16),
        compiler_params=pltpu.CompilerParams(dimension_semantics=("parallel", "parallel"),
                                             vmem_limit_bytes=_vmem_limit(blocks)),
        name="mixer_b_prompt",
    )(sinks.astype(F32), qkv, qkv, qkv, qkv, qkv, bias_b)


def _heads_to_rows(x, n_heads):
    return jnp.concatenate([x[:, h * HEAD_DIM:(h + 1) * HEAD_DIM] for h in range(n_heads)], axis=0)


def _rows_to_heads(x, n_heads):
    s = x.shape[0] // n_heads
    return jnp.concatenate([x[h * s:(h + 1) * s, :] for h in range(n_heads)], axis=1)


def _attn_sample_kernel(qkv_ref, *refs):
    kx_refs, ky_ref = refs[0:SA_X_SPLIT], refs[SA_X_SPLIT]
    vx_refs, vy_ref = refs[SA_X_SPLIT + 1:2 * SA_X_SPLIT + 1], refs[2 * SA_X_SPLIT + 1]
    (kb_ref, vb_ref, mba_ref, mbb_ref, sink_ref, o_ref,
     ka_scr, va_scr, kb_scr, vb_scr, z_scr) = refs[2 * SA_X_SPLIT + 2:]

    def new_rows(col0, n_heads, n_rows):
        new = _heads_to_rows(qkv_ref[:, col0 * LANES:(col0 + n_heads) * LANES], n_heads)
        return jnp.concatenate([new, jnp.zeros((n_rows - new.shape[0], HEAD_DIM), F32)], axis=0).astype(BF16)

    def fill_a(scr, x_refs, y_ref, col0):
        part = SA_X_COLS // SA_X_SPLIT
        for s, x_ref in enumerate(x_refs):
            scr[s * part:(s + 1) * part, :] = x_ref[...].reshape(part, HEAD_DIM).astype(BF16)
        scr[SA_X_COLS:SA_X_COLS + SA_Y_COLS, :] = y_ref[...].reshape(SA_Y_COLS, HEAD_DIM).astype(BF16)
        scr[SA_X_COLS + SA_Y_COLS:SA_COLS, :] = new_rows(col0, N_HEADS_A, SA_NEW_COLS)

    def fill_b(scr, c_ref, col0):
        scr[0:SB_CACHE_COLS, :] = c_ref[...].astype(BF16)
        scr[SB_CACHE_COLS:SB_COLS, :] = new_rows(col0, N_KV_B, SB_NEW_COLS)

    def queries(col0, n_heads):
        q = _heads_to_rows(qkv_ref[:, col0 * LANES:(col0 + n_heads) * LANES], n_heads)
        return (q * QK_SCALE_LOG2).astype(BF16)

    fill_a(ka_scr, kx_refs, ky_ref, COL_KA)
    fill_a(va_scr, vx_refs, vy_ref, COL_VA)
    fill_b(kb_scr, kb_ref, COL_KB)
    fill_b(vb_scr, vb_ref, COL_VB)

    qa = queries(COL_QA, N_HEADS_A)
    m = None
    for c0 in range(0, SA_COLS, SA_CHUNK):
        z = _dot_nt(qa, ka_scr[c0:c0 + SA_CHUNK, :]) + mba_ref[:, c0:c0 + SA_CHUNK]
        z_scr[:, c0:c0 + SA_CHUNK] = z
        zmax = jnp.max(z, axis=-1, keepdims=True)
        m = zmax if m is None else jnp.maximum(m, zmax)
    denom, oa = 0.0, 0.0
    for c0 in range(0, SA_COLS, SA_CHUNK):
        p = jnp.exp2(z_scr[:, c0:c0 + SA_CHUNK] - m)
        denom = denom + jnp.sum(p, axis=-1, keepdims=True)
        oa = oa + jnp.dot(p.astype(BF16), va_scr[c0:c0 + SA_CHUNK, :], preferred_element_type=F32)
    o_ref[:, 0:QA_W] = _rows_to_heads(oa / denom, N_HEADS_A)

    zb = _dot_nt(queries(COL_QB, N_HEADS_B), kb_scr[...]) + mbb_ref[...]
    sink = sink_ref[:, 0:1]
    mb = jnp.maximum(jnp.max(zb, axis=-1, keepdims=True), sink)
    eb = jnp.exp2(zb - mb)
    denom_b = jnp.sum(eb, axis=-1, keepdims=True) + jnp.exp2(sink - mb)
    ob = jnp.dot(eb.astype(BF16), vb_scr[...], preferred_element_type=F32) / denom_b
    o_ref[:, QA_W:QA_W + QB_W] = _rows_to_heads(ob, N_HEADS_B)


def _attn_sample(qkv, layer, cache_ak, cache_av, cache_bk, cache_bv, mb_a, mb_b, sink_rows, s_new):
    depth, db, la = cache_ak.shape[0], cache_ak.shape[1], cache_ak.shape[2]
    lb = cache_bk.shape[2]
    ax_k = cache_ak.reshape(depth, db, la // DIL_MAX, DIL_MAX, N_HEADS_A, HEAD_DIM)
    ax_v = cache_av.reshape(depth, db, la // DIL_MAX, DIL_MAX, N_HEADS_A, HEAD_DIM)
    bk = cache_bk.reshape(depth, db, lb * N_KV_B, HEAD_DIM)
    bv = cache_bv.reshape(depth, db, lb * N_KV_B, HEAD_DIM)
    xblk = (None, None, SA_X_M // SA_X_SPLIT, SUBLANES, N_HEADS_A, HEAD_DIM)
    xspecs = [pl.BlockSpec(xblk, functools.partial(lambda b, s: (layer, b, s, 0, 0, 0), s=s))
              for s in range(SA_X_SPLIT)]
    yblk = (None, None, SA_Y_POS, N_HEADS_A, HEAD_DIM)
    bblk = (None, None, SB_CACHE_COLS, HEAD_DIM)
    y_idx = (la - SA_Y_POS) // SA_Y_POS
    rows_q = N_HEADS_A * s_new
    const = lambda b: (0, 0)
    blocks = (2 * (_nbytes((s_new, IN_W), F32) + 2 * _nbytes((SA_X_COLS + SA_Y_COLS, HEAD_DIM), F32)
                   + 2 * _nbytes((SB_CACHE_COLS, HEAD_DIM), F32)
                   + _nbytes((rows_q, SA_COLS), F32) + _nbytes((rows_q, SB_COLS), F32)
                   + _nbytes((rows_q, LANES), F32) + _nbytes((s_new, QA_W + QB_W), F32))
              + 2 * _nbytes((SA_COLS, HEAD_DIM), BF16) + 2 * _nbytes((SB_COLS, HEAD_DIM), BF16)
              + 4 * _nbytes((rows_q, SA_COLS), F32))
    return pl.pallas_call(
        _attn_sample_kernel,
        grid=(db,),
        in_specs=[pl.BlockSpec((s_new, IN_W), lambda b: (b, 0)),
                  *xspecs,
                  pl.BlockSpec(yblk, lambda b: (layer, b, y_idx, 0, 0)),
                  *xspecs,
                  pl.BlockSpec(yblk, lambda b: (layer, b, y_idx, 0, 0)),
                  pl.BlockSpec(bblk, lambda b: (layer, b, 0, 0)),
                  pl.BlockSpec(bblk, lambda b: (layer, b, 0, 0)),
                  pl.BlockSpec((rows_q, SA_COLS), const),
                  pl.BlockSpec((rows_q, SB_COLS), const),
                  pl.BlockSpec((rows_q, LANES), const)],
        out_specs=pl.BlockSpec((s_new, QA_W + QB_W), lambda b: (b, 0)),
        out_shape=jax.ShapeDtypeStruct((db * s_new, QA_W + QB_W), F32),
        scratch_shapes=[pltpu.VMEM((SA_COLS, HEAD_DIM), BF16), pltpu.VMEM((SA_COLS, HEAD_DIM), BF16),
                        pltpu.VMEM((SB_COLS, HEAD_DIM), BF16), pltpu.VMEM((SB_COLS, HEAD_DIM), BF16),
                        pltpu.VMEM((rows_q, SA_COLS), F32)],
        compiler_params=pltpu.CompilerParams(dimension_semantics=("parallel",),
                                             vmem_limit_bytes=_vmem_limit(blocks)),
        name="mixers_sample",
    )(qkv, *([ax_k] * SA_X_SPLIT), cache_ak, *([ax_v] * SA_X_SPLIT), cache_av, bk, bv, mb_a, mb_b, sink_rows)


def _outproj_kernel(x_ref, ma_ref, mb_ref, wa_ref, wb_ref, gt_ref, o_ref):
    acc = jnp.dot(ma_ref[...].astype(BF16), wa_ref[...], preferred_element_type=F32)
    acc += jnp.dot(mb_ref[...].astype(BF16), wb_ref[...], preferred_element_type=F32)
    o_ref[...] = x_ref[...] + gt_ref[...] * acc.reshape(o_ref.shape)


def _out_projection(x3, mix_a, mix_b, col_a, col_b, layer, w_out_bf, mod3, bb, rb):
    nb, r, d = x3.shape
    tm = bb * rb
    tn = d
    rblocks = r // rb
    n_m = (nb // bb) * rblocks
    kh = QA_W
    blocks = 2 * (2 * _nbytes((tm, tn), F32) + _nbytes((tm, kh), mix_a.dtype) + _nbytes((tm, kh), mix_b.dtype)
                  + 2 * _nbytes((kh, tn), BF16) + _nbytes((bb, SUBLANES, tn), F32))
    return pl.pallas_call(
        _outproj_kernel,
        grid=(n_m, d // tn),
        in_specs=[pl.BlockSpec((bb, rb, tn), lambda i, j: (i // rblocks, i % rblocks, j)),
                  pl.BlockSpec((tm, kh), lambda i, j: (i, col_a)),
                  pl.BlockSpec((tm, kh), lambda i, j: (i, col_b)),
                  pl.BlockSpec((None, kh, tn), lambda i, j: (layer, 0, j)),
                  pl.BlockSpec((None, kh, tn), lambda i, j: (layer, 1, j)),
                  pl.BlockSpec((bb, 1, tn), lambda i, j: (i // rblocks, 0, 2 * (d // tn) + j))],
        out_specs=pl.BlockSpec((bb, rb, tn), lambda i, j: (i // rblocks, i % rblocks, j)),
        out_shape=jax.ShapeDtypeStruct(x3.shape, F32),
        compiler_params=pltpu.CompilerParams(dimension_semantics=("parallel", "parallel"),
                                             vmem_limit_bytes=_vmem_limit(blocks)),
        name="out_projection",
    )(x3, mix_a, mix_b, w_out_bf, w_out_bf, mod3)


def _ffn_kernel(x_ref, sc_ref, sh_ref, gt_ref, g_ref, wg_ref, wu_ref, wd_ref, wc_ref, bc_ref, hist_ref,
                o_ref, cs_ref, h_scr, gate_scr, up_scr, act_scr, carry_scr, *, rblocks, chunk_seqs, chunk_rows):
    i = pl.program_id(0)
    f = pl.program_id(1)
    n_f = pl.num_programs(1)
    bb, rb, d = x_ref.shape
    tm, tf = bb * rb, wg_ref.shape[1]

    @pl.when(f == 0)
    def _():
        _norm_modulate_rows(x_ref, g_ref, sc_ref, sh_ref, h_scr)
        o_ref[...] = jnp.zeros(o_ref.shape, F32)

    if rblocks > 1:
        @pl.when(i % rblocks == 0)
        def _():
            carry_scr[f] = hist_ref[...]
        history = lambda s0, n: carry_scr[f]
    else:
        history = lambda s0, n: hist_ref[s0:s0 + n]

    h = h_scr[...]
    gate_scr[0:SUBLANES, :] = jnp.zeros((SUBLANES, tf), F32)
    gate_scr[SUBLANES:, :] = jnp.dot(h, wg_ref[...], preferred_element_type=F32)
    up_scr[...] = jnp.dot(h, wu_ref[...], preferred_element_type=F32)

    n_rows = chunk_seqs * chunk_rows
    shape = (chunk_seqs, chunk_rows, tf)
    t = lax.broadcasted_iota(jnp.int32, shape, 1)
    for r0 in range(0, tm, n_rows):
        def shifted(s):
            lo = SUBLANES + r0 - s
            return gate_scr[lo:lo + n_rows, :].reshape(shape)
        g0, g1, g2 = shifted(0), shifted(1), shifted(2)
        if r0 % rb == 0:
            hist = history(r0 // rb, chunk_seqs)
            g1 = jnp.where(t >= 1, g1, hist[:, 1:2, :])
            g2 = jnp.where(t >= 2, g2, jnp.where(t == 0, hist[:, 0:1, :], hist[:, 1:2, :]))
        gc = bc_ref[...] + wc_ref[0:1, :] * g2 + wc_ref[1:2, :] * g1 + wc_ref[2:3, :] * g0
        act = jax.nn.gelu(gc, approximate=True) * up_scr[r0:r0 + n_rows, :].reshape(shape)
        act_scr[r0:r0 + n_rows, :] = act.reshape(n_rows, tf).astype(BF16)
        if (r0 + n_rows) % rb == 0:
            s0 = (r0 + n_rows) // rb - chunk_seqs
            tail = g0[:, chunk_rows - (CONV_W - 1):, :]
            cs_ref[s0:s0 + chunk_seqs] = tail
            if rblocks > 1:
                carry_scr[f] = tail
    o_ref[...] += jnp.dot(act_scr[...], wd_ref[...], preferred_element_type=F32).reshape(bb, rb, d)

    @pl.when(f == n_f - 1)
    def _():
        o_ref[...] = x_ref[...] + gt_ref[...] * o_ref[...]


FFN_CHUNK_ROWS = 64


def _conv_ffn(x3, mod3, layer, g_ffn, wg_bf, wu_bf, wd_bf, w_conv, b_conv, hist, bb, rb):
    nb, r, d = x3.shape
    f_pad = wg_bf.shape[2]
    tf = FF_TILE
    n_f = f_pad // tf
    tm = bb * rb
    rblocks = r // rb
    assert rblocks == 1 or bb == 1
    n_m = (nb // bb) * rblocks
    carry_slots = n_f if rblocks > 1 else 1
    chunk_rows = min(rb, FFN_CHUNK_ROWS)
    chunk_seqs = FFN_CHUNK_ROWS // chunk_rows
    assert rb % chunk_rows == 0 and bb % chunk_seqs == 0 and chunk_rows >= CONV_W - 1
    modmap = lambda k: (lambda i, f: (i // rblocks, 0, k))
    blocks = (2 * (2 * _nbytes((tm, d), F32) + 2 * _nbytes((d, tf), BF16) + _nbytes((tf, d), BF16)
                   + 2 * _nbytes((bb, SUBLANES, tf), F32) + 3 * _nbytes((bb, SUBLANES, d), F32))
              + _nbytes((tm, d), BF16) + carry_slots * _nbytes((bb, SUBLANES, tf), F32)
              + 3 * _nbytes((tm + SUBLANES, tf), F32))
    y, tails = pl.pallas_call(
        functools.partial(_ffn_kernel, rblocks=rblocks, chunk_seqs=chunk_seqs, chunk_rows=chunk_rows),
        grid=(n_m, n_f),
        in_specs=[pl.BlockSpec((bb, rb, d), lambda i, f: (i // rblocks, i % rblocks, 0)),
                  pl.BlockSpec((bb, 1, d), modmap(4)),
                  pl.BlockSpec((bb, 1, d), modmap(3)),
                  pl.BlockSpec((bb, 1, d), modmap(5)),
                  pl.BlockSpec((1, d), lambda i, f: (0, 0)),
                  pl.BlockSpec((None, d, tf), lambda i, f: (layer, 0, f)),
                  pl.BlockSpec((None, d, tf), lambda i, f: (layer, 0, f)),
                  pl.BlockSpec((None, tf, d), lambda i, f: (layer, f, 0)),
                  pl.BlockSpec((None, CONV_W, tf), lambda i, f: (layer, 0, f)),
                  pl.BlockSpec((None, 1, tf), lambda i, f: (layer, 0, f)),
                  pl.BlockSpec((bb, CONV_W - 1, tf), lambda i, f: (i // rblocks, 0, f))],
        out_specs=[pl.BlockSpec((bb, rb, d), lambda i, f: (i // rblocks, i % rblocks, 0)),
                   pl.BlockSpec((bb, CONV_W - 1, tf), lambda i, f: (i, 0, f))],
        out_shape=[jax.ShapeDtypeStruct(x3.shape, F32),
                   jax.ShapeDtypeStruct((n_m * bb, CONV_W - 1, f_pad), F32)],
        scratch_shapes=[pltpu.VMEM((tm, d), BF16),
                        pltpu.VMEM((tm + SUBLANES, tf), F32),
                        pltpu.VMEM((tm, tf), F32),
                        pltpu.VMEM((tm, tf), BF16),
                        pltpu.VMEM((carry_slots, bb, CONV_W - 1, tf), F32)],
        compiler_params=pltpu.CompilerParams(dimension_semantics=("arbitrary", "arbitrary"),
                                             vmem_limit_bytes=_vmem_limit(blocks)),
        name="conv_ffn",
    )(x3, mod3, mod3, mod3, g_ffn.reshape(1, d), wg_bf, wu_bf, wd_bf, w_conv, b_conv, hist)
    state = tails.reshape(nb // bb, rblocks, bb, CONV_W - 1, f_pad)[:, rblocks - 1]
    return y, state.reshape(nb, CONV_W - 1, f_pad)


def _pad_last(a, n):
    return jnp.pad(a, [(0, 0)] * (a.ndim - 1) + [(0, n - a.shape[-1])])


def _cast_pad_cols_kernel(x_ref, o_ref):
    n = x_ref.shape[1]
    o_ref[:, :n] = x_ref[...].astype(BF16)
    o_ref[:, n:] = jnp.zeros((o_ref.shape[0], o_ref.shape[1] - n), BF16)


def _cast_pad_cols(w, n_pad):
    depth, rows, n = w.shape
    tr = 256
    blocks = 2 * (_nbytes((tr, n), F32) + _nbytes((tr, n_pad), BF16))
    return pl.pallas_call(
        _cast_pad_cols_kernel,
        grid=(depth, rows // tr),
        in_specs=[pl.BlockSpec((None, tr, n), lambda l, i: (l, i, 0))],
        out_specs=pl.BlockSpec((None, tr, n_pad), lambda l, i: (l, i, 0)),
        out_shape=jax.ShapeDtypeStruct((depth, rows, n_pad), BF16),
        compiler_params=pltpu.CompilerParams(dimension_semantics=("parallel", "parallel"),
                                             vmem_limit_bytes=_vmem_limit(blocks)),
        name="cast_pad_cols",
    )(w)


def kernel(x_prompt, x_sample, c_prompt, c_sample, cache_a_k, cache_a_v, cache_b_k, cache_b_v, state_conv, rel_bias, w_ada, b_ada, g_attn, g_ffn, w_in, g_qn_a, g_kn_a, g_qn_b, g_kn_b, sinks, w_out, w_gate, w_up, w_conv, b_conv, w_down):
    depth = w_in.shape[0]
    batch, seq, d = x_prompt.shape
    db, s_new, _ = x_sample.shape
    d_ff = w_gate.shape[2]
    f_pad = -(-d_ff // FF_TILE) * FF_TILE
    la, lb = cache_a_k.shape[2], cache_b_k.shape[2]
    assert batch == 1 and seq % CHUNK_A == 0 and s_new == SUBLANES

    pb_rev = _bias_by_distance_rev(rel_bias, max(WIN_A + 1, la + s_new)) * LOG2E
    bias_a, bias_b = _prompt_bias_tables(pb_rev)
    mb_a, mb_b = _sample_tables(pb_rev, s_new, la, lb)
    sinks2 = sinks.astype(F32) * LOG2E

    c_all = jnp.concatenate([c_prompt, jnp.zeros((SUBLANES - batch, d), F32), c_sample], axis=0)

    ones = jnp.ones((HEAD_DIM,), F32)
    flag = jnp.concatenate([jnp.ones((2 * QA_W,), F32), jnp.zeros((QA_W,), F32), jnp.ones((QB_W + KB_W,), F32),
                            jnp.zeros((KB_W,), F32)]).reshape(1, IN_W)

    w_in_bf = w_in.astype(BF16)
    w_out_bf = w_out.astype(BF16)
    wg_bf = _cast_pad_cols(w_gate, f_pad)
    wu_bf = _cast_pad_cols(w_up, f_pad)
    wd_bf = jnp.pad(w_down.astype(BF16), ((0, 0), (0, f_pad - d_ff), (0, 0)))
    wc = _pad_last(w_conv, f_pad)
    bc = _pad_last(b_conv, f_pad).reshape(depth, 1, f_pad)
    mod_all = _modulation(c_all, w_ada, b_ada)

    xp = x_prompt
    xs = x_sample
    hist_p = jnp.zeros((batch, CONV_W - 1, f_pad), F32)
    outs = [[] for _ in range(10)]
    for l in range(depth):
        gain = jnp.concatenate([jnp.tile(g_qn_a[l], N_HEADS_A), jnp.tile(g_kn_a[l], N_HEADS_A),
                                jnp.tile(ones, N_HEADS_A), jnp.tile(g_qn_b[l], N_HEADS_B),
                                jnp.tile(g_kn_b[l], N_KV_B), jnp.tile(ones, N_KV_B)]).reshape(1, IN_W)
        sink_rows = jnp.broadcast_to(jnp.repeat(sinks2[l], s_new)[:, None], (N_HEADS_B * s_new, LANES))
        mod_p = mod_all[l, 0:batch].reshape(batch, 1, 6 * d)
        mod_s = mod_all[l, SUBLANES:].reshape(db, 1, 6 * d)

        qkv_p = _projection(xp, mod_p, l, g_attn[l], w_in_bf, gain, flag, 1, 1024)
        mix_pa = _attn_a_prompt(qkv_p, bias_a)
        mix_pb = _attn_b_prompt(qkv_p, bias_b, sinks2[l])
        xp = _out_projection(xp, mix_pa, mix_pb, 0, 0, l, w_out_bf, mod_p, 1, 512)
        xp, conv_p = _conv_ffn(xp, mod_p, l, g_ffn[l], wg_bf, wu_bf, wd_bf, wc, bc, hist_p, 1, 1024)
        keep_a, keep_b = min(WIN_A, seq), min(WIN_B, seq)
        outs[0].append(qkv_p[seq - keep_a:, COL_KA * LANES:COL_KA * LANES + QA_W].reshape(batch, keep_a, N_HEADS_A, HEAD_DIM))
        outs[1].append(qkv_p[seq - keep_a:, COL_VA * LANES:COL_VA * LANES + QA_W].reshape(batch, keep_a, N_HEADS_A, HEAD_DIM))
        outs[2].append(qkv_p[seq - keep_b:, COL_KB * LANES:COL_KB * LANES + KB_W].reshape(batch, keep_b, N_KV_B, HEAD_DIM))
        outs[3].append(qkv_p[seq - keep_b:, COL_VB * LANES:COL_VB * LANES + KB_W].reshape(batch, keep_b, N_KV_B, HEAD_DIM))
        outs[4].append(conv_p[:, :, :d_ff])

        qkv_s = _projection(xs, mod_s, l, g_attn[l], w_in_bf, gain, flag, db, s_new)
        mix_s = _attn_sample(qkv_s, l, cache_a_k, cache_a_v, cache_b_k, cache_b_v,
                             mb_a, mb_b, sink_rows, s_new)
        xs = _out_projection(xs, mix_s, mix_s, 0, 1, l, w_out_bf, mod_s, db // 2, s_new)
        xs, conv_s = _conv_ffn(xs, mod_s, l, g_ffn[l], wg_bf, wu_bf, wd_bf, wc, bc,
                               _pad_last(state_conv[l], f_pad), db // 2, s_new)
        outs[5].append(qkv_s[:, COL_KA * LANES:COL_KA * LANES + QA_W].reshape(db, s_new, N_HEADS_A, HEAD_DIM))
        outs[6].append(qkv_s[:, COL_VA * LANES:COL_VA * LANES + QA_W].reshape(db, s_new, N_HEADS_A, HEAD_DIM))
        outs[7].append(qkv_s[:, COL_KB * LANES:COL_KB * LANES + KB_W].reshape(db, s_new, N_KV_B, HEAD_DIM))
        outs[8].append(qkv_s[:, COL_VB * LANES:COL_VB * LANES + KB_W].reshape(db, s_new, N_KV_B, HEAD_DIM))
        outs[9].append(conv_s[:, :, :d_ff])

    return (xp, xs) + tuple(jnp.stack(o) for o in outs)
```

```python
import functools
import math

import numpy as np
import jax
import jax.numpy as jnp
from jax import lax
from jax.experimental import pallas as pl
from jax.experimental.pallas import tpu as pltpu

F32 = jnp.float32
BF16 = jnp.bfloat16

LANES = 128
SUBLANES = 8
VMEM_BYTES_V7X = 64 * 1024 * 1024

HEAD_DIM = 128
N_HEADS_A = 8
N_HEADS_B = 8
N_KV_B = 2
GROUP_B = N_HEADS_B // N_KV_B
A_BRANCHES = ((128, 1), (512, 4), (2048, 16))
WIN_A = 2048
WIN_B = 128
BLOCK = 128
N_BUCKETS = 32
MAX_DISTANCE = 2048
CONV_W = 3
EPS = 1e-6
NEG = -1e30
SCALE = HEAD_DIM ** -0.5
LOG2E = math.log2(math.e)
QK_SCALE_LOG2 = SCALE * LOG2E

QA_W = N_HEADS_A * HEAD_DIM
QB_W = N_HEADS_B * HEAD_DIM
KB_W = N_KV_B * HEAD_DIM
IN_W = 3 * QA_W + QB_W + 2 * KB_W
COL_QA, COL_KA, COL_VA = 0, QA_W // LANES, 2 * QA_W // LANES
COL_QB = 3 * QA_W // LANES
COL_KB = COL_QB + QB_W // LANES
COL_VB = COL_KB + KB_W // LANES

CHUNK_A = WIN_A
CHUNK_B = 1024
FF_TILE = 512


def _vmem_limit(block_bytes):
    return int(min(VMEM_BYTES_V7X - (4 << 20), block_bytes + (12 << 20)))


def _nbytes(shape, dtype):
    return int(np.prod(shape)) * jnp.dtype(dtype).itemsize


def _t5_bucket_np(dist):
    dist = np.maximum(dist, 0)
    max_exact = N_BUCKETS // 2
    ratio = np.log(np.maximum(dist, 1).astype(np.float64) / max_exact) / math.log(MAX_DISTANCE / max_exact)
    large = max_exact + (ratio * (N_BUCKETS - max_exact)).astype(np.int32)
    large = np.minimum(large, N_BUCKETS - 1)
    return np.where(dist < max_exact, dist, large).astype(np.int32)


def _bias_by_distance_rev(rel_bias, n):
    buckets = _t5_bucket_np(np.arange(n))
    assert np.all(np.diff(buckets) >= 0)
    runs = np.bincount(buckets, minlength=N_BUCKETS)
    tab = rel_bias.astype(F32).T
    return jnp.concatenate([jnp.broadcast_to(tab[:, k:k + 1], (tab.shape[0], int(runs[k])))
                            for k in reversed(range(N_BUCKETS)) if runs[k] > 0], axis=1)


def _banded_table(vals_rev):
    hh, n = vals_rev.shape
    period = 4 * BLOCK
    w = jnp.concatenate([jnp.full((hh, BLOCK + 1 - n), NEG, F32), vals_rev,
                         jnp.full((hh, period - (BLOCK + 1)), NEG, F32)], axis=1)
    skew = jnp.tile(w, (1, BLOCK))[:, :BLOCK * (period - 1)].reshape(hh, BLOCK, period - 1)
    return skew[:, :, :2 * BLOCK]


def _prompt_bias_tables(pb_rev):
    n = pb_rev.shape[1]
    bias_a = jnp.stack([_banded_table(pb_rev[:N_HEADS_A, n - 1 - BLOCK * dil::dil]) for _, dil in A_BRANCHES])
    bias_b = _banded_table(pb_rev[N_HEADS_A:, n - WIN_B:])
    return bias_a, bias_b.reshape(N_KV_B, GROUP_B, BLOCK, 2 * BLOCK)


DIL_MAX = A_BRANCHES[-1][1]
DIL_MID = A_BRANCHES[1][1]
assert A_BRANCHES[0][1] == 1 and DIL_MAX == DIL_MID ** 2
SA_Y_POS = A_BRANCHES[1][0]
SA_X_M = (WIN_A - SA_Y_POS) // DIL_MAX
SA_ROWS_PER_STEP = 2
SA_X_COLS = SA_X_M * SUBLANES * N_HEADS_A
SA_Y_COLS = SA_Y_POS * N_HEADS_A
SA_CHUNK = 768
SA_NEW_COLS = -(SA_X_COLS + SA_Y_COLS + N_HEADS_A * SUBLANES) % SA_CHUNK + N_HEADS_A * SUBLANES
SA_COLS = SA_X_COLS + SA_Y_COLS + SA_NEW_COLS
assert SA_COLS % SA_CHUNK == 0 and SA_NEW_COLS % (2 * SUBLANES) == 0
SB_CACHE_COLS = WIN_B * N_KV_B
SB_NEW_COLS = LANES
SB_COLS = SB_CACHE_COLS + SB_NEW_COLS


def _sample_columns(s_new, la, lb):
    m, res, h = np.meshgrid(np.arange(SA_X_M), np.arange(SUBLANES), np.arange(N_HEADS_A), indexing="ij")
    pos_x, head_x = (DIL_MAX * m + res).reshape(-1), h.reshape(-1)
    p, h = np.meshgrid(np.arange(la - SA_Y_POS, la), np.arange(N_HEADS_A), indexing="ij")
    pos_y, head_y = p.reshape(-1), h.reshape(-1)
    h, t = np.meshgrid(np.arange(N_HEADS_A), np.arange(s_new), indexing="ij")
    pad = np.full((SA_NEW_COLS - N_HEADS_A * s_new,), -1)
    pos_a = np.concatenate([pos_x, pos_y, la + t.reshape(-1), pad])
    head_a = np.concatenate([head_x, head_y, h.reshape(-1), pad])
    p, g = np.meshgrid(np.arange(lb), np.arange(N_KV_B), indexing="ij")
    g2, t = np.meshgrid(np.arange(N_KV_B), np.arange(s_new), indexing="ij")
    pad = np.full((SB_NEW_COLS - N_KV_B * s_new,), -1)
    pos_b = np.concatenate([p.reshape(-1), lb + t.reshape(-1), pad])
    head_b = np.concatenate([g.reshape(-1), g2.reshape(-1), pad])
    return pos_a, head_a, pos_b, head_b


def _query_key_bias_t(pbh_rev, length, s_new):
    n = length + s_new
    rev = jnp.pad(pbh_rev[:, pbh_rev.shape[1] - n:], ((0, 0), (0, s_new - 1)))
    a = jnp.stack([rev[:, s_new - 1 - i:s_new - 1 - i + n] for i in range(s_new)], axis=1)
    return a.reshape(-1, n).T


def _key_rows(at, pos_major, n_key_heads):
    return jnp.repeat(at, n_key_heads, axis=0) if pos_major else jnp.tile(at, (n_key_heads, 1))


def _sample_tables(pb_rev, s_new, la, lb):
    assert la == WIN_A and lb == WIN_B and s_new == SUBLANES
    pos_a, head_a, pos_b, head_b = _sample_columns(s_new, la, lb)
    qh, qi = np.meshgrid(np.arange(N_HEADS_A), np.arange(s_new), indexing="ij")
    qh, qi = qh.reshape(-1, 1), qi.reshape(-1, 1)

    dist = la + qi - pos_a[None, :]
    real = (pos_a >= 0)[None, :] & (dist >= 0) & (head_a[None, :] == qh)
    count = np.zeros(dist.shape, np.float32)
    for win, dil in A_BRANCHES:
        count += (real & (dist % dil == 0) & (dist <= win)).astype(np.float32)
    n_q = N_HEADS_A * s_new
    at = _query_key_bias_t(pb_rev[:N_HEADS_A], la, s_new)
    at_x = at[:la - SA_Y_POS].reshape(SA_X_M, DIL_MAX, n_q)[:, :SUBLANES].reshape(-1, n_q)
    rows_a = jnp.concatenate([
        _key_rows(at_x, True, N_HEADS_A),
        _key_rows(at[la - SA_Y_POS:la], True, N_HEADS_A),
        _key_rows(at[la:], False, N_HEADS_A),
        jnp.zeros((SA_NEW_COLS - N_HEADS_A * s_new, n_q), F32)], axis=0)
    log2_count = np.log2(np.maximum(count, 1.0)).T
    mb_a = jnp.where(jnp.asarray(count.T > 0), rows_a + jnp.asarray(log2_count), NEG).T

    dist_b = lb + qi - pos_b[None, :]
    valid_b = (pos_b >= 0)[None, :] & (dist_b >= 0) & (dist_b < WIN_B) & (head_b[None, :] == qh // GROUP_B)
    bt = _query_key_bias_t(pb_rev[N_HEADS_A:], lb, s_new)
    rows_b = jnp.concatenate([
        _key_rows(bt[:lb], True, N_KV_B),
        _key_rows(bt[lb:], False, N_KV_B),
        jnp.zeros((SB_NEW_COLS - N_KV_B * s_new, n_q), F32)], axis=0)
    mb_b = jnp.where(jnp.asarray(valid_b.T), rows_b, NEG).T
    return mb_a, mb_b


def _split_bf16(a):
    hi = a.astype(BF16)
    lo = (a - hi.astype(F32)).astype(BF16)
    return hi, lo


def _mod_kernel(c_ref, w_ref, b_ref, o_ref):
    c = c_ref[...]
    s_hi, s_lo = _split_bf16(c * jax.nn.sigmoid(c))
    w_hi, w_lo = _split_bf16(w_ref[...])
    acc = jnp.dot(s_hi, w_hi, preferred_element_type=F32)
    acc += jnp.dot(s_hi, w_lo, preferred_element_type=F32)
    acc += jnp.dot(s_lo, w_hi, preferred_element_type=F32)
    o_ref[...] = acc + b_ref[...]


def _modulation(c_all, w_ada, b_ada):
    m, d = c_all.shape
    depth, _, n = w_ada.shape
    tn = 1024
    blocks = 2 * (_nbytes((d, tn), F32) + _nbytes((m, tn), F32)) + _nbytes((m, d), F32) * 2 + 2 * _nbytes((d, tn), BF16)
    return pl.pallas_call(
        _mod_kernel,
        grid=(depth, n // tn),
        in_specs=[pl.BlockSpec((m, d), lambda l, j: (0, 0)),
                  pl.BlockSpec((None, d, tn), lambda l, j: (l, 0, j)),
                  pl.BlockSpec((None, 1, tn), lambda l, j: (l, 0, j))],
        out_specs=pl.BlockSpec((None, m, tn), lambda l, j: (l, 0, j)),
        out_shape=jax.ShapeDtypeStruct((depth, m, n), F32),
        compiler_params=pltpu.CompilerParams(dimension_semantics=("parallel", "parallel"),
                                             vmem_limit_bytes=_vmem_limit(blocks)),
        name="modulation",
    )(c_all, w_ada, b_ada.reshape(depth, 1, n))


def _norm_modulate(x, g, sc, sh):
    ms = jnp.mean(x * x, axis=-1, keepdims=True)
    return (x * lax.rsqrt(ms + EPS) * g) * (1.0 + sc) + sh


NORM_ROWS = 16


def _norm_modulate_rows(x_ref, g_ref, sc_ref, sh_ref, h_scr):
    bb, rb, d = x_ref.shape
    cr = min(rb, NORM_ROWS)
    cs = NORM_ROWS // cr
    for r0 in range(0, bb * rb, NORM_ROWS):
        s, t0 = r0 // rb, r0 % rb
        h = _norm_modulate(x_ref[s:s + cs, t0:t0 + cr, :], g_ref[...], sc_ref[s:s + cs], sh_ref[s:s + cs])
        h_scr[r0:r0 + NORM_ROWS, :] = h.reshape(NORM_ROWS, d).astype(BF16)


def _dot_nt(a, b):
    return lax.dot_general(a, b, (((1,), (1,)), ((), ())), preferred_element_type=F32)


def _proj_kernel(x_ref, sc_ref, sh_ref, g_ref, w_ref, gain_ref, flag_ref, o_ref, h_scr, acc_scr):
    @pl.when(pl.program_id(1) == 0)
    def _():
        _norm_modulate_rows(x_ref, g_ref, sc_ref, sh_ref, h_scr)

    acc_scr[...] = jnp.dot(h_scr[...], w_ref[...], preferred_element_type=F32)
    tm, tn = acc_scr.shape
    for r0 in range(0, tm, 64):
        for c in range(tn // LANES):
            sl = slice(c * LANES, (c + 1) * LANES)
            blk = acc_scr[r0:r0 + 64, sl]
            ms = jnp.mean(blk * blk, axis=-1, keepdims=True)
            nrm = blk * lax.rsqrt(ms + EPS) * gain_ref[:, sl]
            o_ref[r0:r0 + 64, sl] = jnp.where(flag_ref[:, sl] > 0.0, nrm, blk)


def _projection(x3, mod3, layer, g_attn, w_in_bf, gain, flag, bb, rb):
    nb, r, d = x3.shape
    tm = bb * rb
    tn = 768
    n_m = (nb // bb) * (r // rb)
    rblocks = r // rb
    xmap = lambda i, j: (i // rblocks, i % rblocks, 0)
    blocks = (2 * (_nbytes((tm, d), F32) + _nbytes((d, tn), BF16) + _nbytes((tm, tn), F32))
              + _nbytes((tm, d), BF16) + 8 * _nbytes((bb, SUBLANES, d), F32) + _nbytes((tm, tn), F32))
    return pl.pallas_call(
        _proj_kernel,
        grid=(n_m, IN_W // tn),
        in_specs=[pl.BlockSpec((bb, rb, d), xmap),
                  pl.BlockSpec((bb, 1, d), lambda i, j: (i // rblocks, 0, 1)),
                  pl.BlockSpec((bb, 1, d), lambda i, j: (i // rblocks, 0, 0)),
                  pl.BlockSpec((1, d), lambda i, j: (0, 0)),
                  pl.BlockSpec((None, d, tn), lambda i, j: (layer, 0, j)),
                  pl.BlockSpec((1, tn), lambda i, j: (0, j)),
                  pl.BlockSpec((1, tn), lambda i, j: (0, j))],
        out_specs=pl.BlockSpec((tm, tn), lambda i, j: (i, j)),
        out_shape=jax.ShapeDtypeStruct((nb * r, IN_W), F32),
        scratch_shapes=[pltpu.VMEM((tm, d), BF16), pltpu.VMEM((tm, tn), F32)],
        compiler_params=pltpu.CompilerParams(dimension_semantics=("parallel", "arbitrary"),
                                             vmem_limit_bytes=_vmem_limit(blocks)),
        name="projection",
    )(x3, mod3, mod3, g_attn.reshape(1, d), w_in_bf, gain, flag)


def _attn_a_prompt_kernel(q_ref, kc_ref, kp_ref, vc_ref, vp_ref, bias_ref, o_ref, num_scr, m_scr, s_scr, d4_scr):
    first_chunk = pl.program_id(0) == 0
    chunk = q_ref.shape[0]
    plane = chunk // DIL_MID
    srcs = (q_ref, kc_ref, kp_ref, vc_ref, vp_ref)
    for a, ref in enumerate(srcs):
        for p in range(DIL_MID):
            d4_scr[a, p] = ref[pl.ds(p, plane, stride=DIL_MID), :]

    def plane_rows(r, ub, dil):
        step = dil // DIL_MID
        lo = r // DIL_MID + step * BLOCK * ub
        return r % DIL_MID, (slice(lo, lo + BLOCK) if step == 1 else pl.ds(lo, BLOCK, stride=step))

    def tile(a, r, ub, dil):
        if dil == 1:
            return srcs[a][ub * BLOCK:(ub + 1) * BLOCK, :]
        p, rows = plane_rows(r, ub, dil)
        return d4_scr[a, p, rows, :]

    col = lax.broadcasted_iota(jnp.int32, (BLOCK, 2 * BLOCK), 1)
    ones = jnp.ones((2 * BLOCK, LANES), BF16)
    for bi, (_, dil) in enumerate(A_BRANCHES):
        nsub = chunk // (BLOCK * dil)
        bias = bias_ref[bi]
        for r in range(dil):
            k_prev = tile(2, r, nsub - 1, dil).astype(BF16)
            v_prev = tile(4, r, nsub - 1, dil).astype(BF16)
            for ub in range(nsub):
                q = (tile(0, r, ub, dil) * QK_SCALE_LOG2).astype(BF16)
                k_cur = tile(1, r, ub, dil).astype(BF16)
                v_cur = tile(3, r, ub, dil).astype(BF16)
                z = _dot_nt(q, jnp.concatenate([k_prev, k_cur], axis=0)) + bias
                if ub == 0:
                    z = jnp.where(jnp.logical_and(first_chunk, col < BLOCK), NEG, z)
                m = jnp.max(z, axis=-1, keepdims=True)
                e = jnp.exp2(z - m).astype(BF16)
                v_ext = jnp.concatenate([jnp.concatenate([v_prev, v_cur], axis=0), ones], axis=1)
                num = jnp.dot(e, v_ext, preferred_element_type=F32)
                if dil == 1:
                    rows = slice(ub * BLOCK, (ub + 1) * BLOCK)
                else:
                    p, rows = plane_rows(r, ub, dil)
                    rows = (slice(p * plane + rows.start, p * plane + rows.stop) if isinstance(rows, slice)
                            else pl.ds(p * plane + rows.start, BLOCK, stride=rows.stride))
                num_scr[bi, rows, :] = num[:, :HEAD_DIM]
                m_scr[bi, rows, :] = jnp.broadcast_to(m, (BLOCK, LANES))
                s_scr[bi, rows, :] = num[:, HEAD_DIM:]
                k_prev, v_prev = k_cur, v_cur
    quarter = plane // DIL_MID
    for p in range(DIL_MID):
        def stat(scr, bi):
            if bi == 0:
                return scr[0, pl.ds(p, plane, stride=DIL_MID), :]
            return scr[bi, p * plane:(p + 1) * plane, :]
        ms = [stat(m_scr, bi) for bi in range(len(A_BRANCHES))]
        m_all = functools.reduce(jnp.maximum, ms)
        top = jnp.zeros_like(m_all)
        bot = jnp.zeros_like(m_all)
        for bi in range(len(A_BRANCHES)):
            w = jnp.exp2(ms[bi] - m_all)
            top += w * stat(num_scr, bi)
            bot += w * stat(s_scr, bi)
        out = top / bot
        for t4 in range(DIL_MID):
            d4_scr[0, t4, pl.ds(p, quarter, stride=DIL_MID), :] = out[t4 * quarter:(t4 + 1) * quarter]
    for t4 in range(DIL_MID):
        o_ref[t4 * plane:(t4 + 1) * plane, :] = d4_scr[0, t4].astype(o_ref.dtype)


def _attn_a_prompt(qkv, bias_a):
    t = qkv.shape[0]
    n_chunks = t // CHUNK_A
    blk = (CHUNK_A, HEAD_DIM)
    prev = lambda b: jnp.maximum(b - 1, 0)
    blocks = (2 * (5 * _nbytes(blk, F32) + _nbytes((3, BLOCK, 2 * BLOCK), F32) + _nbytes(blk, BF16))
              + 14 * _nbytes(blk, F32))
    return pl.pallas_call(
        _attn_a_prompt_kernel,
        grid=(n_chunks, N_HEADS_A),
        in_specs=[pl.BlockSpec(blk, lambda b, h: (b, COL_QA + h)),
                  pl.BlockSpec(blk, lambda b, h: (b, COL_KA + h)),
                  pl.BlockSpec(blk, lambda b, h: (prev(b), COL_KA + h)),
                  pl.BlockSpec(blk, lambda b, h: (b, COL_VA + h)),
                  pl.BlockSpec(blk, lambda b, h: (prev(b), COL_VA + h)),
                  pl.BlockSpec((3, None, BLOCK, 2 * BLOCK), lambda b, h: (0, h, 0, 0))],
        out_specs=pl.BlockSpec(blk, lambda b, h: (b, h)),
        out_shape=jax.ShapeDtypeStruct((t, QA_W), BF16),
        scratch_shapes=[pltpu.VMEM((3, CHUNK_A, HEAD_DIM), F32)] * 3
        + [pltpu.VMEM((5, DIL_MID, CHUNK_A // DIL_MID, HEAD_DIM), F32)],
        compiler_params=pltpu.CompilerParams(dimension_semantics=("parallel", "parallel"),
                                             vmem_limit_bytes=_vmem_limit(blocks)),
        name="mixer_a_prompt",
    )(qkv, qkv, qkv, qkv, qkv, bias_a)


def _attn_b_prompt_kernel(sink_ref, q_ref, kc_ref, kp_ref, vc_ref, vp_ref, bias_ref, o_ref):
    first_chunk = pl.program_id(0) == 0
    g = pl.program_id(1)
    col = lax.broadcasted_iota(jnp.int32, (BLOCK, 2 * BLOCK), 1)
    k_prev = kp_ref[...].astype(BF16)
    v_prev = vp_ref[...].astype(BF16)
    for blk in range(q_ref.shape[0] // BLOCK):
        rows = slice(blk * BLOCK, (blk + 1) * BLOCK)
        k_cur = kc_ref[rows, :].astype(BF16)
        v_cur = vc_ref[rows, :].astype(BF16)
        kcat = jnp.concatenate([k_prev, k_cur], axis=0)
        vcat = jnp.concatenate([v_prev, v_cur], axis=0)
        for j in range(GROUP_B):
            cols = slice(j * HEAD_DIM, (j + 1) * HEAD_DIM)
            sink = sink_ref[g * GROUP_B + j]
            z = _dot_nt((q_ref[rows, cols] * QK_SCALE_LOG2).astype(BF16), kcat) + bias_ref[j]
            if blk == 0:
                z = jnp.where(jnp.logical_and(first_chunk, col < BLOCK), NEG, z)
            m = jnp.maximum(jnp.max(z, axis=-1, keepdims=True), sink)
            e = jnp.exp2(z - m)
            denom = jnp.sum(e, axis=-1, keepdims=True) + jnp.exp2(sink - m)
            num = jnp.dot(e.astype(BF16), vcat, preferred_element_type=F32)
            o_ref[rows, cols] = (num / denom).astype(o_ref.dtype)
        k_prev, v_prev = k_cur, v_cur


def _attn_b_prompt(qkv, bias_b, sinks):
    t = qkv.shape[0]
    n_chunks = t // CHUNK_B
    per = CHUNK_B // BLOCK
    qblk = (CHUNK_B, GROUP_B * HEAD_DIM)
    kblk = (CHUNK_B, HEAD_DIM)
    pblk = (BLOCK, HEAD_DIM)
    qcol = COL_QB // GROUP_B
    prev = lambda b: jnp.maximum(b * per - 1, 0)
    blocks = 2 * (_nbytes(qblk, F32) + 2 * _nbytes(kblk, F32) + 2 * _nbytes(pblk, F32)
                  + _nbytes((GROUP_B, BLOCK, 2 * BLOCK), F32) + _nbytes(qblk, BF16))
    return pl.pallas_call(
        _attn_b_prompt_kernel,
        grid=(n_chunks, N_KV_B),
        in_specs=[pl.BlockSpec(memory_space=pltpu.SMEM),
                  pl.BlockSpec(qblk, lambda b, g: (b, qcol + g)),
                  pl.BlockSpec(kblk, lambda b, g: (b, COL_KB + g)),
                  pl.BlockSpec(pblk, lambda b, g: (prev(b), COL_KB + g)),
                  pl.BlockSpec(kblk, lambda b, g: (b, COL_VB + g)),
                  pl.BlockSpec(pblk, lambda b, g: (prev(b), COL_VB + g)),
                  pl.BlockSpec((None, GROUP_B, BLOCK, 2 * BLOCK), lambda b, g: (g, 0, 0, 0))],
        out_specs=pl.BlockSpec(qblk, lambda b, g: (b, g)),
        out_shape=jax.ShapeDtypeStruct((t, QB_W), BF16),
        compiler_params=pltpu.CompilerParams(dimension_semantics=("parallel", "parallel"),
                                             vmem_limit_bytes=_vmem_limit(blocks)),
        name="mixer_b_prompt",
    )(sinks.astype(F32), qkv, qkv, qkv, qkv, qkv, bias_b)


def _heads_to_rows(x, n_heads):
    return jnp.concatenate([x[:, h * HEAD_DIM:(h + 1) * HEAD_DIM] for h in range(n_heads)], axis=0)


def _rows_to_heads(x, n_heads):
    s = x.shape[0] // n_heads
    return jnp.concatenate([x[h * s:(h + 1) * s, :] for h in range(n_heads)], axis=1)


def _attn_sample_kernel(qkv_ref, kx_ref, ky_ref, vx_ref, vy_ref, kb_ref, vb_ref,
                        mba_ref, mbb_ref, sink_ref, o_ref,
                        ka_scr, va_scr, kb_scr, vb_scr, z_scr):
    s_new = qkv_ref.shape[0] // SA_ROWS_PER_STEP
    for row in range(SA_ROWS_PER_STEP):
        tok = slice(row * s_new, (row + 1) * s_new)

        def new_rows(col0, n_heads, n_rows):
            new = _heads_to_rows(qkv_ref[tok, col0 * LANES:(col0 + n_heads) * LANES], n_heads)
            return jnp.concatenate([new, jnp.zeros((n_rows - new.shape[0], HEAD_DIM), F32)], axis=0).astype(BF16)

        def fill_a(scr, x_ref, y_ref, col0):
            scr[0:SA_X_COLS, :] = x_ref[row].reshape(SA_X_COLS, HEAD_DIM).astype(BF16)
            scr[SA_X_COLS:SA_X_COLS + SA_Y_COLS, :] = y_ref[row].reshape(SA_Y_COLS, HEAD_DIM).astype(BF16)
            scr[SA_X_COLS + SA_Y_COLS:SA_COLS, :] = new_rows(col0, N_HEADS_A, SA_NEW_COLS)

        def fill_b(scr, c_ref, col0):
            scr[0:SB_CACHE_COLS, :] = c_ref[row].astype(BF16)
            scr[SB_CACHE_COLS:SB_COLS, :] = new_rows(col0, N_KV_B, SB_NEW_COLS)

        def queries(col0, n_heads):
            q = _heads_to_rows(qkv_ref[tok, col0 * LANES:(col0 + n_heads) * LANES], n_heads)
            return (q * QK_SCALE_LOG2).astype(BF16)

        fill_a(ka_scr, kx_ref, ky_ref, COL_KA)
        fill_a(va_scr, vx_ref, vy_ref, COL_VA)
        fill_b(kb_scr, kb_ref, COL_KB)
        fill_b(vb_scr, vb_ref, COL_VB)

        qa = queries(COL_QA, N_HEADS_A)
        m = None
        for c0 in range(0, SA_COLS, SA_CHUNK):
            z = _dot_nt(qa, ka_scr[c0:c0 + SA_CHUNK, :]) + mba_ref[:, c0:c0 + SA_CHUNK]
            z_scr[:, c0:c0 + SA_CHUNK] = z
            zmax = jnp.max(z, axis=-1, keepdims=True)
            m = zmax if m is None else jnp.maximum(m, zmax)
        denom, oa = 0.0, 0.0
        for c0 in range(0, SA_COLS, SA_CHUNK):
            p = jnp.exp2(z_scr[:, c0:c0 + SA_CHUNK] - m)
            denom = denom + jnp.sum(p, axis=-1, keepdims=True)
            oa = oa + jnp.dot(p.astype(BF16), va_scr[c0:c0 + SA_CHUNK, :], preferred_element_type=F32)
        o_ref[tok, 0:QA_W] = _rows_to_heads(oa / denom, N_HEADS_A)

        zb = _dot_nt(queries(COL_QB, N_HEADS_B), kb_scr[...]) + mbb_ref[...]
        sink = sink_ref[:, 0:1]
        mb = jnp.maximum(jnp.max(zb, axis=-1, keepdims=True), sink)
        eb = jnp.exp2(zb - mb)
        denom_b = jnp.sum(eb, axis=-1, keepdims=True) + jnp.exp2(sink - mb)
        ob = jnp.dot(eb.astype(BF16), vb_scr[...], preferred_element_type=F32) / denom_b
        o_ref[tok, QA_W:QA_W + QB_W] = _rows_to_heads(ob, N_HEADS_B)


def _attn_sample(qkv, layer, cache_ak, cache_av, cache_bk, cache_bv, mb_a, mb_b, sink_rows, s_new):
    depth, db, la = cache_ak.shape[0], cache_ak.shape[1], cache_ak.shape[2]
    lb = cache_bk.shape[2]
    rps = SA_ROWS_PER_STEP
    ax_k = cache_ak.reshape(depth, db, la // DIL_MAX, DIL_MAX, N_HEADS_A, HEAD_DIM)
    ax_v = cache_av.reshape(depth, db, la // DIL_MAX, DIL_MAX, N_HEADS_A, HEAD_DIM)
    bk = cache_bk.reshape(depth, db, lb * N_KV_B, HEAD_DIM)
    bv = cache_bv.reshape(depth, db, lb * N_KV_B, HEAD_DIM)
    xspec = pl.BlockSpec((None, rps, SA_X_M, SUBLANES, N_HEADS_A, HEAD_DIM), lambda b: (layer, b, 0, 0, 0, 0))
    yspec = pl.BlockSpec((None, rps, SA_Y_POS, N_HEADS_A, HEAD_DIM),
                         lambda b: (layer, b, (la - SA_Y_POS) // SA_Y_POS, 0, 0))
    bspec = pl.BlockSpec((None, rps, SB_CACHE_COLS, HEAD_DIM), lambda b: (layer, b, 0, 0))
    rows_q = N_HEADS_A * s_new
    const = lambda b: (0, 0)
    blocks = (2 * rps * (_nbytes((s_new, IN_W), F32) + 2 * _nbytes((SA_X_COLS + SA_Y_COLS, HEAD_DIM), F32)
                         + 2 * _nbytes((SB_CACHE_COLS, HEAD_DIM), F32) + _nbytes((s_new, QA_W + QB_W), F32))
              + 2 * (_nbytes((rows_q, SA_COLS), F32) + _nbytes((rows_q, SB_COLS), F32) + _nbytes((rows_q, LANES), F32))
              + 2 * _nbytes((SA_COLS, HEAD_DIM), BF16) + 2 * _nbytes((SB_COLS, HEAD_DIM), BF16)
              + _nbytes((rows_q, SA_COLS), F32))
    return pl.pallas_call(
        _attn_sample_kernel,
        grid=(db // rps,),
        in_specs=[pl.BlockSpec((rps * s_new, IN_W), lambda b: (b, 0)),
                  xspec, yspec, xspec, yspec, bspec, bspec,
                  pl.BlockSpec((rows_q, SA_COLS), const),
                  pl.BlockSpec((rows_q, SB_COLS), const),
                  pl.BlockSpec((rows_q, LANES), const)],
        out_specs=pl.BlockSpec((rps * s_new, QA_W + QB_W), lambda b: (b, 0)),
        out_shape=jax.ShapeDtypeStruct((db * s_new, QA_W + QB_W), F32),
        scratch_shapes=[pltpu.VMEM((SA_COLS, HEAD_DIM), BF16), pltpu.VMEM((SA_COLS, HEAD_DIM), BF16),
                        pltpu.VMEM((SB_COLS, HEAD_DIM), BF16), pltpu.VMEM((SB_COLS, HEAD_DIM), BF16),
                        pltpu.VMEM((rows_q, SA_COLS), F32)],
        compiler_params=pltpu.CompilerParams(dimension_semantics=("parallel",),
                                             vmem_limit_bytes=_vmem_limit(blocks)),
        name="mixers_sample",
    )(qkv, ax_k, cache_ak, ax_v, cache_av, bk, bv, mb_a, mb_b, sink_rows)


def _outproj_kernel(x_ref, ma_ref, mb_ref, wa_ref, wb_ref, gt_ref, o_ref):
    acc = jnp.dot(ma_ref[...].astype(BF16), wa_ref[...], preferred_element_type=F32)
    acc += jnp.dot(mb_ref[...].astype(BF16), wb_ref[...], preferred_element_type=F32)
    o_ref[...] = x_ref[...] + gt_ref[...] * acc.reshape(o_ref.shape)


def _out_projection(x3, mix_a, mix_b, col_a, col_b, layer, w_out_bf, mod3, bb, rb):
    nb, r, d = x3.shape
    tm = bb * rb
    tn = d
    rblocks = r // rb
    n_m = (nb // bb) * rblocks
    kh = QA_W
    blocks = 2 * (2 * _nbytes((tm, tn), F32) + _nbytes((tm, kh), mix_a.dtype) + _nbytes((tm, kh), mix_b.dtype)
                  + 2 * _nbytes((kh, tn), BF16) + _nbytes((bb, SUBLANES, tn), F32))
    return pl.pallas_call(
        _outproj_kernel,
        grid=(n_m, d // tn),
        in_specs=[pl.BlockSpec((bb, rb, tn), lambda i, j: (i // rblocks, i % rblocks, j)),
                  pl.BlockSpec((tm, kh), lambda i, j: (i, col_a)),
                  pl.BlockSpec((tm, kh), lambda i, j: (i, col_b)),
                  pl.BlockSpec((None, kh, tn), lambda i, j: (layer, 0, j)),
                  pl.BlockSpec((None, kh, tn), lambda i, j: (layer, 1, j)),
                  pl.BlockSpec((bb, 1, tn), lambda i, j: (i // rblocks, 0, 2 * (d // tn) + j))],
        out_specs=pl.BlockSpec((bb, rb, tn), lambda i, j: (i // rblocks, i % rblocks, j)),
        out_shape=jax.ShapeDtypeStruct(x3.shape, F32),
        compiler_params=pltpu.CompilerParams(dimension_semantics=("parallel", "parallel"),
                                             vmem_limit_bytes=_vmem_limit(blocks)),
        name="out_projection",
    )(x3, mix_a, mix_b, w_out_bf, w_out_bf, mod3)


def _ffn_kernel(x_ref, sc_ref, sh_ref, gt_ref, g_ref, wg_ref, wu_ref, wd_ref, wc_ref, bc_ref, hist_ref,
                o_ref, cs_ref, h_scr, gate_scr, up_scr, act_scr, carry_scr, *, rblocks, chunk_seqs, chunk_rows):
    i = pl.program_id(0)
    f = pl.program_id(1)
    n_f = pl.num_programs(1)
    bb, rb, d = x_ref.shape
    tm, tf = bb * rb, wg_ref.shape[1]

    @pl.when(f == 0)
    def _():
        _norm_modulate_rows(x_ref, g_ref, sc_ref, sh_ref, h_scr)
        o_ref[...] = jnp.zeros(o_ref.shape, F32)

    if rblocks > 1:
        @pl.when(i % rblocks == 0)
        def _():
            carry_scr[f] = hist_ref[...]
        history = lambda s0, n: carry_scr[f]
    else:
        history = lambda s0, n: hist_ref[s0:s0 + n]

    h = h_scr[...]
    gate_scr[0:SUBLANES, :] = jnp.zeros((SUBLANES, tf), F32)
    gate_scr[SUBLANES:, :] = jnp.dot(h, wg_ref[...], preferred_element_type=F32)
    up_scr[...] = jnp.dot(h, wu_ref[...], preferred_element_type=F32)

    n_rows = chunk_seqs * chunk_rows
    shape = (chunk_seqs, chunk_rows, tf)
    t = lax.broadcasted_iota(jnp.int32, shape, 1)
    for r0 in range(0, tm, n_rows):
        def shifted(s):
            lo = SUBLANES + r0 - s
            return gate_scr[lo:lo + n_rows, :].reshape(shape)
        g0, g1, g2 = shifted(0), shifted(1), shifted(2)
        if r0 % rb == 0:
            hist = history(r0 // rb, chunk_seqs)
            g1 = jnp.where(t >= 1, g1, hist[:, 1:2, :])
            g2 = jnp.where(t >= 2, g2, jnp.where(t == 0, hist[:, 0:1, :], hist[:, 1:2, :]))
        gc = bc_ref[...] + wc_ref[0:1, :] * g2 + wc_ref[1:2, :] * g1 + wc_ref[2:3, :] * g0
        act = jax.nn.gelu(gc, approximate=True) * up_scr[r0:r0 + n_rows, :].reshape(shape)
        act_scr[r0:r0 + n_rows, :] = act.reshape(n_rows, tf).astype(BF16)
        if (r0 + n_rows) % rb == 0:
            s0 = (r0 + n_rows) // rb - chunk_seqs
            tail = g0[:, chunk_rows - (CONV_W - 1):, :]
            cs_ref[s0:s0 + chunk_seqs] = tail
            if rblocks > 1:
                carry_scr[f] = tail
    o_ref[...] += jnp.dot(act_scr[...], wd_ref[...], preferred_element_type=F32).reshape(bb, rb, d)

    @pl.when(f == n_f - 1)
    def _():
        o_ref[...] = x_ref[...] + gt_ref[...] * o_ref[...]


FFN_CHUNK_ROWS = 64


def _conv_ffn(x3, mod3, layer, g_ffn, wg_bf, wu_bf, wd_bf, w_conv, b_conv, hist, bb, rb):
    nb, r, d = x3.shape
    f_pad = wg_bf.shape[2]
    tf = FF_TILE
    n_f = f_pad // tf
    tm = bb * rb
    rblocks = r // rb
    assert rblocks == 1 or bb == 1
    n_m = (nb // bb) * rblocks
    carry_slots = n_f if rblocks > 1 else 1
    chunk_rows = min(rb, FFN_CHUNK_ROWS)
    chunk_seqs = FFN_CHUNK_ROWS // chunk_rows
    assert rb % chunk_rows == 0 and bb % chunk_seqs == 0 and chunk_rows >= CONV_W - 1
    modmap = lambda k: (lambda i, f: (i // rblocks, 0, k))
    blocks = (2 * (2 * _nbytes((tm, d), F32) + 2 * _nbytes((d, tf), BF16) + _nbytes((tf, d), BF16)
                   + 2 * _nbytes((bb, SUBLANES, tf), F32) + 3 * _nbytes((bb, SUBLANES, d), F32))
              + _nbytes((tm, d), BF16) + carry_slots * _nbytes((bb, SUBLANES, tf), F32)
              + 3 * _nbytes((tm + SUBLANES, tf), F32))
    y, tails = pl.pallas_call(
        functools.partial(_ffn_kernel, rblocks=rblocks, chunk_seqs=chunk_seqs, chunk_rows=chunk_rows),
        grid=(n_m, n_f),
        in_specs=[pl.BlockSpec((bb, rb, d), lambda i, f: (i // rblocks, i % rblocks, 0)),
                  pl.BlockSpec((bb, 1, d), modmap(4)),
                  pl.BlockSpec((bb, 1, d), modmap(3)),
                  pl.BlockSpec((bb, 1, d), modmap(5)),
                  pl.BlockSpec((1, d), lambda i, f: (0, 0)),
                  pl.BlockSpec((None, d, tf), lambda i, f: (layer, 0, f)),
                  pl.BlockSpec((None, d, tf), lambda i, f: (layer, 0, f)),
                  pl.BlockSpec((None, tf, d), lambda i, f: (layer, f, 0)),
                  pl.BlockSpec((None, CONV_W, tf), lambda i, f: (layer, 0, f)),
                  pl.BlockSpec((None, 1, tf), lambda i, f: (layer, 0, f)),
                  pl.BlockSpec((bb, CONV_W - 1, tf), lambda i, f: (i // rblocks, 0, f))],
        out_specs=[pl.BlockSpec((bb, rb, d), lambda i, f: (i // rblocks, i % rblocks, 0)),
                   pl.BlockSpec((bb, CONV_W - 1, tf), lambda i, f: (i, 0, f))],
        out_shape=[jax.ShapeDtypeStruct(x3.shape, F32),
                   jax.ShapeDtypeStruct((n_m * bb, CONV_W - 1, f_pad), F32)],
        scratch_shapes=[pltpu.VMEM((tm, d), BF16),
                        pltpu.VMEM((tm + SUBLANES, tf), F32),
                        pltpu.VMEM((tm, tf), F32),
                        pltpu.VMEM((tm, tf), BF16),
                        pltpu.VMEM((carry_slots, bb, CONV_W - 1, tf), F32)],
        compiler_params=pltpu.CompilerParams(dimension_semantics=("arbitrary", "arbitrary"),
                                             vmem_limit_bytes=_vmem_limit(blocks)),
        name="conv_ffn",
    )(x3, mod3, mod3, mod3, g_ffn.reshape(1, d), wg_bf, wu_bf, wd_bf, w_conv, b_conv, hist)
    state = tails.reshape(nb // bb, rblocks, bb, CONV_W - 1, f_pad)[:, rblocks - 1]
    return y, state.reshape(nb, CONV_W - 1, f_pad)


def _pad_last(a, n):
    return jnp.pad(a, [(0, 0)] * (a.ndim - 1) + [(0, n - a.shape[-1])])


def _cast_pad_cols_kernel(x_ref, o_ref):
    n = x_ref.shape[1]
    o_ref[:, :n] = x_ref[...].astype(BF16)
    o_ref[:, n:] = jnp.zeros((o_ref.shape[0], o_ref.shape[1] - n), BF16)


def _cast_pad_cols(w, n_pad):
    depth, rows, n = w.shape
    tr = 256
    blocks = 2 * (_nbytes((tr, n), F32) + _nbytes((tr, n_pad), BF16))
    return pl.pallas_call(
        _cast_pad_cols_kernel,
        grid=(depth, rows // tr),
        in_specs=[pl.BlockSpec((None, tr, n), lambda l, i: (l, i, 0))],
        out_specs=pl.BlockSpec((None, tr, n_pad), lambda l, i: (l, i, 0)),
        out_shape=jax.ShapeDtypeStruct((depth, rows, n_pad), BF16),
        compiler_params=pltpu.CompilerParams(dimension_semantics=("parallel", "parallel"),
                                             vmem_limit_bytes=_vmem_limit(blocks)),
        name="cast_pad_cols",
    )(w)


def kernel(x_prompt, x_sample, c_prompt, c_sample, cache_a_k, cache_a_v, cache_b_k, cache_b_v, state_conv, rel_bias, w_ada, b_ada, g_attn, g_ffn, w_in, g_qn_a, g_kn_a, g_qn_b, g_kn_b, sinks, w_out, w_gate, w_up, w_conv, b_conv, w_down):
    depth = w_in.shape[0]
    batch, seq, d = x_prompt.shape
    db, s_new, _ = x_sample.shape
    d_ff = w_gate.shape[2]
    f_pad = -(-d_ff // FF_TILE) * FF_TILE
    la, lb = cache_a_k.shape[2], cache_b_k.shape[2]
    assert batch == 1 and seq % CHUNK_A == 0 and s_new == SUBLANES

    pb_rev = _bias_by_distance_rev(rel_bias, max(WIN_A + 1, la + s_new)) * LOG2E
    bias_a, bias_b = _prompt_bias_tables(pb_rev)
    mb_a, mb_b = _sample_tables(pb_rev, s_new, la, lb)
    sinks2 = sinks.astype(F32) * LOG2E

    c_all = jnp.concatenate([c_prompt, jnp.zeros((SUBLANES - batch, d), F32), c_sample], axis=0)

    ones = jnp.ones((HEAD_DIM,), F32)
    flag = jnp.concatenate([jnp.ones((2 * QA_W,), F32), jnp.zeros((QA_W,), F32), jnp.ones((QB_W + KB_W,), F32),
                            jnp.zeros((KB_W,), F32)]).reshape(1, IN_W)

    w_in_bf = w_in.astype(BF16)
    w_out_bf = w_out.astype(BF16)
    wg_bf = _cast_pad_cols(w_gate, f_pad)
    wu_bf = _cast_pad_cols(w_up, f_pad)
    wd_bf = jnp.pad(w_down.astype(BF16), ((0, 0), (0, f_pad - d_ff), (0, 0)))
    wc = _pad_last(w_conv, f_pad)
    bc = _pad_last(b_conv, f_pad).reshape(depth, 1, f_pad)
    mod_all = _modulation(c_all, w_ada, b_ada)

    xp = x_prompt
    xs = x_sample
    hist_p = jnp.zeros((batch, CONV_W - 1, f_pad), F32)
    outs = [[] for _ in range(10)]
    for l in range(depth):
        gain = jnp.concatenate([jnp.tile(g_qn_a[l], N_HEADS_A), jnp.tile(g_kn_a[l], N_HEADS_A),
                                jnp.tile(ones, N_HEADS_A), jnp.tile(g_qn_b[l], N_HEADS_B),
                                jnp.tile(g_kn_b[l], N_KV_B), jnp.tile(ones, N_KV_B)]).reshape(1, IN_W)
        sink_rows = jnp.broadcast_to(jnp.repeat(sinks2[l], s_new)[:, None], (N_HEADS_B * s_new, LANES))
        mod_p = mod_all[l, 0:batch].reshape(batch, 1, 6 * d)
        mod_s = mod_all[l, SUBLANES:].reshape(db, 1, 6 * d)

        qkv_p = _projection(xp, mod_p, l, g_attn[l], w_in_bf, gain, flag, 1, 1024)
        mix_pa = _attn_a_prompt(qkv_p, bias_a)
        mix_pb = _attn_b_prompt(qkv_p, bias_b, sinks2[l])
        xp = _out_projection(xp, mix_pa, mix_pb, 0, 0, l, w_out_bf, mod_p, 1, 512)
        xp, conv_p = _conv_ffn(xp, mod_p, l, g_ffn[l], wg_bf, wu_bf, wd_bf, wc, bc, hist_p, 1, 1024)
        keep_a, keep_b = min(WIN_A, seq), min(WIN_B, seq)
        outs[0].append(qkv_p[seq - keep_a:, COL_KA * LANES:COL_KA * LANES + QA_W].reshape(batch, keep_a, N_HEADS_A, HEAD_DIM))
        outs[1].append(qkv_p[seq - keep_a:, COL_VA * LANES:COL_VA * LANES + QA_W].reshape(batch, keep_a, N_HEADS_A, HEAD_DIM))
        outs[2].append(qkv_p[seq - keep_b:, COL_KB * LANES:COL_KB * LANES + KB_W].reshape(batch, keep_b, N_KV_B, HEAD_DIM))
        outs[3].append(qkv_p[seq - keep_b:, COL_VB * LANES:COL_VB * LANES + KB_W].reshape(batch, keep_b, N_KV_B, HEAD_DIM))
        outs[4].append(conv_p[:, :, :d_ff])

        qkv_s = _projection(xs, mod_s, l, g_attn[l], w_in_bf, gain, flag, db, s_new)
        mix_s = _attn_sample(qkv_s, l, cache_a_k, cache_a_v, cache_b_k, cache_b_v,
                             mb_a, mb_b, sink_rows, s_new)
        xs = _out_projection(xs, mix_s, mix_s, 0, 1, l, w_out_bf, mod_s, db // 2, s_new)
        xs, conv_s = _conv_ffn(xs, mod_s, l, g_ffn[l], wg_bf, wu_bf, wd_bf, wc, bc,
                               _pad_last(state_conv[l], f_pad), db // 2, s_new)
        outs[5].append(qkv_s[:, COL_KA * LANES:COL_KA * LANES + QA_W].reshape(db, s_new, N_HEADS_A, HEAD_DIM))
        outs[6].append(qkv_s[:, COL_VA * LANES:COL_VA * LANES + QA_W].reshape(db, s_new, N_HEADS_A, HEAD_DIM))
        outs[7].append(qkv_s[:, COL_KB * LANES:COL_KB * LANES + KB_W].reshape(db, s_new, N_KV_B, HEAD_DIM))
        outs[8].append(qkv_s[:, COL_VB * LANES:COL_VB * LANES + KB_W].reshape(db, s_new, N_KV_B, HEAD_DIM))
        outs[9].append(conv_s[:, :, :d_ff])

    return (xp, xs) + tuple(jnp.stack(o) for o in outs)
```

```python
import functools
import math

import numpy as np
import jax
import jax.numpy as jnp
from jax import lax
from jax.experimental import pallas as pl
from jax.experimental.pallas import tpu as pltpu

F32 = jnp.float32
BF16 = jnp.bfloat16

LANES = 128
SUBLANES = 8
VMEM_BYTES_V7X = 64 * 1024 * 1024

HEAD_DIM = 128
N_HEADS_A = 8
N_HEADS_B = 8
N_KV_B = 2
GROUP_B = N_HEADS_B // N_KV_B
A_BRANCHES = ((128, 1), (512, 4), (2048, 16))
WIN_A = 2048
WIN_B = 128
BLOCK = 128
N_BUCKETS = 32
MAX_DISTANCE = 2048
CONV_W = 3
EPS = 1e-6
NEG = -1e30
SCALE = HEAD_DIM ** -0.5
LOG2E = math.log2(math.e)
QK_SCALE_LOG2 = SCALE * LOG2E

QA_W = N_HEADS_A * HEAD_DIM
QB_W = N_HEADS_B * HEAD_DIM
KB_W = N_KV_B * HEAD_DIM
IN_W = 3 * QA_W + QB_W + 2 * KB_W
COL_QA, COL_KA, COL_VA = 0, QA_W // LANES, 2 * QA_W // LANES
COL_QB = 3 * QA_W // LANES
COL_KB = COL_QB + QB_W // LANES
COL_VB = COL_KB + KB_W // LANES

CHUNK_A = WIN_A
CHUNK_B = 2048
FF_TILE = 512


def _vmem_limit(block_bytes):
    return int(min(VMEM_BYTES_V7X - (4 << 20), block_bytes + (12 << 20)))


def _nbytes(shape, dtype):
    return int(np.prod(shape)) * jnp.dtype(dtype).itemsize


def _t5_bucket_np(dist):
    dist = np.maximum(dist, 0)
    max_exact = N_BUCKETS // 2
    ratio = np.log(np.maximum(dist, 1).astype(np.float64) / max_exact) / math.log(MAX_DISTANCE / max_exact)
    large = max_exact + (ratio * (N_BUCKETS - max_exact)).astype(np.int32)
    large = np.minimum(large, N_BUCKETS - 1)
    return np.where(dist < max_exact, dist, large).astype(np.int32)


def _bias_by_distance_rev(rel_bias, n):
    buckets = _t5_bucket_np(np.arange(n))
    assert np.all(np.diff(buckets) >= 0)
    runs = np.bincount(buckets, minlength=N_BUCKETS)
    tab = rel_bias.astype(F32).T
    return jnp.concatenate([jnp.broadcast_to(tab[:, k:k + 1], (tab.shape[0], int(runs[k])))
                            for k in reversed(range(N_BUCKETS)) if runs[k] > 0], axis=1)


def _banded_table(vals_rev):
    hh, n = vals_rev.shape
    period = 4 * BLOCK
    w = jnp.concatenate([jnp.full((hh, BLOCK + 1 - n), NEG, F32), vals_rev,
                         jnp.full((hh, period - (BLOCK + 1)), NEG, F32)], axis=1)
    skew = jnp.tile(w, (1, BLOCK))[:, :BLOCK * (period - 1)].reshape(hh, BLOCK, period - 1)
    return skew[:, :, :2 * BLOCK]


def _prompt_bias_tables(pb_rev):
    n = pb_rev.shape[1]
    bias_a = jnp.stack([_banded_table(pb_rev[:N_HEADS_A, n - 1 - BLOCK * dil::dil]) for _, dil in A_BRANCHES])
    bias_b = _banded_table(pb_rev[N_HEADS_A:, n - WIN_B:])
    return bias_a, bias_b.reshape(N_KV_B, GROUP_B, BLOCK, 2 * BLOCK)


DIL_MAX = A_BRANCHES[-1][1]
DIL_MID = A_BRANCHES[1][1]
assert A_BRANCHES[0][1] == 1 and DIL_MAX == DIL_MID ** 2
SA_Y_POS = A_BRANCHES[1][0]
SA_X_M = (WIN_A - SA_Y_POS) // DIL_MAX
SA_ROWS_PER_STEP = 2
SA_X_COLS = SA_X_M * SUBLANES * N_HEADS_A
SA_Y_COLS = SA_Y_POS * N_HEADS_A
SA_CHUNK = 768
SA_NEW_COLS = -(SA_X_COLS + SA_Y_COLS + N_HEADS_A * SUBLANES) % SA_CHUNK + N_HEADS_A * SUBLANES
SA_COLS = SA_X_COLS + SA_Y_COLS + SA_NEW_COLS
assert SA_COLS % SA_CHUNK == 0 and SA_NEW_COLS % (2 * SUBLANES) == 0
SB_CACHE_COLS = WIN_B * N_KV_B
SB_NEW_COLS = LANES
SB_COLS = SB_CACHE_COLS + SB_NEW_COLS


def _sample_columns(s_new, la, lb):
    m, res, h = np.meshgrid(np.arange(SA_X_M), np.arange(SUBLANES), np.arange(N_HEADS_A), indexing="ij")
    pos_x, head_x = (DIL_MAX * m + res).reshape(-1), h.reshape(-1)
    p, h = np.meshgrid(np.arange(la - SA_Y_POS, la), np.arange(N_HEADS_A), indexing="ij")
    pos_y, head_y = p.reshape(-1), h.reshape(-1)
    h, t = np.meshgrid(np.arange(N_HEADS_A), np.arange(s_new), indexing="ij")
    pad = np.full((SA_NEW_COLS - N_HEADS_A * s_new,), -1)
    pos_a = np.concatenate([pos_x, pos_y, la + t.reshape(-1), pad])
    head_a = np.concatenate([head_x, head_y, h.reshape(-1), pad])
    p, g = np.meshgrid(np.arange(lb), np.arange(N_KV_B), indexing="ij")
    g2, t = np.meshgrid(np.arange(N_KV_B), np.arange(s_new), indexing="ij")
    pad = np.full((SB_NEW_COLS - N_KV_B * s_new,), -1)
    pos_b = np.concatenate([p.reshape(-1), lb + t.reshape(-1), pad])
    head_b = np.concatenate([g.reshape(-1), g2.reshape(-1), pad])
    return pos_a, head_a, pos_b, head_b


def _query_key_bias_t(pbh_rev, length, s_new):
    n = length + s_new
    rev = jnp.pad(pbh_rev[:, pbh_rev.shape[1] - n:], ((0, 0), (0, s_new - 1)))
    a = jnp.stack([rev[:, s_new - 1 - i:s_new - 1 - i + n] for i in range(s_new)], axis=1)
    return a.reshape(-1, n).T


def _key_rows(at, pos_major, n_key_heads):
    return jnp.repeat(at, n_key_heads, axis=0) if pos_major else jnp.tile(at, (n_key_heads, 1))


def _sample_tables(pb_rev, s_new, la, lb):
    assert la == WIN_A and lb == WIN_B and s_new == SUBLANES
    pos_a, head_a, pos_b, head_b = _sample_columns(s_new, la, lb)
    qh, qi = np.meshgrid(np.arange(N_HEADS_A), np.arange(s_new), indexing="ij")
    qh, qi = qh.reshape(-1, 1), qi.reshape(-1, 1)

    dist = la + qi - pos_a[None, :]
    real = (pos_a >= 0)[None, :] & (dist >= 0) & (head_a[None, :] == qh)
    count = np.zeros(dist.shape, np.float32)
    for win, dil in A_BRANCHES:
        count += (real & (dist % dil == 0) & (dist <= win)).astype(np.float32)
    n_q = N_HEADS_A * s_new
    at = _query_key_bias_t(pb_rev[:N_HEADS_A], la, s_new)
    at_x = at[:la - SA_Y_POS].reshape(SA_X_M, DIL_MAX, n_q)[:, :SUBLANES].reshape(-1, n_q)
    rows_a = jnp.concatenate([
        _key_rows(at_x, True, N_HEADS_A),
        _key_rows(at[la - SA_Y_POS:la], True, N_HEADS_A),
        _key_rows(at[la:], False, N_HEADS_A),
        jnp.zeros((SA_NEW_COLS - N_HEADS_A * s_new, n_q), F32)], axis=0)
    log2_count = np.log2(np.maximum(count, 1.0)).T
    mb_a = jnp.where(jnp.asarray(count.T > 0), rows_a + jnp.asarray(log2_count), NEG).T

    dist_b = lb + qi - pos_b[None, :]
    valid_b = (pos_b >= 0)[None, :] & (dist_b >= 0) & (dist_b < WIN_B) & (head_b[None, :] == qh // GROUP_B)
    bt = _query_key_bias_t(pb_rev[N_HEADS_A:], lb, s_new)
    rows_b = jnp.concatenate([
        _key_rows(bt[:lb], True, N_KV_B),
        _key_rows(bt[lb:], False, N_KV_B),
        jnp.zeros((SB_NEW_COLS - N_KV_B * s_new, n_q), F32)], axis=0)
    mb_b = jnp.where(jnp.asarray(valid_b.T), rows_b, NEG).T
    return mb_a, mb_b


def _split_bf16(a):
    hi = a.astype(BF16)
    lo = (a - hi.astype(F32)).astype(BF16)
    return hi, lo


def _mod_kernel(c_ref, w_ref, b_ref, o_ref):
    c = c_ref[...]
    s_hi, s_lo = _split_bf16(c * jax.nn.sigmoid(c))
    w_hi, w_lo = _split_bf16(w_ref[...])
    acc = jnp.dot(s_hi, w_hi, preferred_element_type=F32)
    acc += jnp.dot(s_hi, w_lo, preferred_element_type=F32)
    acc += jnp.dot(s_lo, w_hi, preferred_element_type=F32)
    o_ref[...] = acc + b_ref[...]


def _modulation(c_all, w_ada, b_ada):
    m, d = c_all.shape
    depth, _, n = w_ada.shape
    tn = 1024
    blocks = 2 * (_nbytes((d, tn), F32) + _nbytes((m, tn), F32)) + _nbytes((m, d), F32) * 2 + 2 * _nbytes((d, tn), BF16)
    return pl.pallas_call(
        _mod_kernel,
        grid=(depth, n // tn),
        in_specs=[pl.BlockSpec((m, d), lambda l, j: (0, 0)),
                  pl.BlockSpec((None, d, tn), lambda l, j: (l, 0, j)),
                  pl.BlockSpec((None, 1, tn), lambda l, j: (l, 0, j))],
        out_specs=pl.BlockSpec((None, m, tn), lambda l, j: (l, 0, j)),
        out_shape=jax.ShapeDtypeStruct((depth, m, n), F32),
        compiler_params=pltpu.CompilerParams(dimension_semantics=("parallel", "parallel"),
                                             vmem_limit_bytes=_vmem_limit(blocks)),
        name="modulation",
    )(c_all, w_ada, b_ada.reshape(depth, 1, n))


def _norm_modulate(x, g, sc, sh):
    ms = jnp.mean(x * x, axis=-1, keepdims=True)
    return (x * lax.rsqrt(ms + EPS) * g) * (1.0 + sc) + sh


NORM_ROWS = 16


def _norm_modulate_rows(x_ref, g_ref, sc_ref, sh_ref, h_scr):
    bb, rb, d = x_ref.shape
    cr = min(rb, NORM_ROWS)
    cs = NORM_ROWS // cr
    for r0 in range(0, bb * rb, NORM_ROWS):
        s, t0 = r0 // rb, r0 % rb
        h = _norm_modulate(x_ref[s:s + cs, t0:t0 + cr, :], g_ref[...], sc_ref[s:s + cs], sh_ref[s:s + cs])
        h_scr[r0:r0 + NORM_ROWS, :] = h.reshape(NORM_ROWS, d).astype(BF16)


def _dot_nt(a, b):
    return lax.dot_general(a, b, (((1,), (1,)), ((), ())), preferred_element_type=F32)


def _proj_kernel(x_ref, sc_ref, sh_ref, g_ref, w_ref, gain_ref, flag_ref, o_ref, h_scr, acc_scr):
    @pl.when(pl.program_id(1) == 0)
    def _():
        _norm_modulate_rows(x_ref, g_ref, sc_ref, sh_ref, h_scr)

    acc_scr[...] = jnp.dot(h_scr[...], w_ref[...], preferred_element_type=F32)
    tm, tn = acc_scr.shape
    for r0 in range(0, tm, 64):
        for c in range(tn // LANES):
            sl = slice(c * LANES, (c + 1) * LANES)
            blk = acc_scr[r0:r0 + 64, sl]
            ms = jnp.mean(blk * blk, axis=-1, keepdims=True)
            nrm = blk * lax.rsqrt(ms + EPS) * gain_ref[:, sl]
            o_ref[r0:r0 + 64, sl] = jnp.where(flag_ref[:, sl] > 0.0, nrm, blk)


def _projection(x3, mod3, layer, g_attn, w_in_bf, gain, flag, bb, rb):
    nb, r, d = x3.shape
    tm = bb * rb
    tn = 768
    n_m = (nb // bb) * (r // rb)
    rblocks = r // rb
    xmap = lambda i, j: (i // rblocks, i % rblocks, 0)
    blocks = (2 * (_nbytes((tm, d), F32) + _nbytes((d, tn), BF16) + _nbytes((tm, tn), F32))
              + _nbytes((tm, d), BF16) + 8 * _nbytes((bb, SUBLANES, d), F32) + _nbytes((tm, tn), F32))
    return pl.pallas_call(
        _proj_kernel,
        grid=(n_m, IN_W // tn),
        in_specs=[pl.BlockSpec((bb, rb, d), xmap),
                  pl.BlockSpec((bb, 1, d), lambda i, j: (i // rblocks, 0, 1)),
                  pl.BlockSpec((bb, 1, d), lambda i, j: (i // rblocks, 0, 0)),
                  pl.BlockSpec((1, d), lambda i, j: (0, 0)),
                  pl.BlockSpec((None, d, tn), lambda i, j: (layer, 0, j)),
                  pl.BlockSpec((1, tn), lambda i, j: (0, j)),
                  pl.BlockSpec((1, tn), lambda i, j: (0, j))],
        out_specs=pl.BlockSpec((tm, tn), lambda i, j: (i, j)),
        out_shape=jax.ShapeDtypeStruct((nb * r, IN_W), F32),
        scratch_shapes=[pltpu.VMEM((tm, d), BF16), pltpu.VMEM((tm, tn), F32)],
        compiler_params=pltpu.CompilerParams(dimension_semantics=("parallel", "arbitrary"),
                                             vmem_limit_bytes=_vmem_limit(blocks)),
        name="projection",
    )(x3, mod3, mod3, g_attn.reshape(1, d), w_in_bf, gain, flag)


def _attn_a_prompt_kernel(q_ref, kc_ref, kp_ref, vc_ref, vp_ref, bias_ref, o_ref, num_scr, m_scr, s_scr, d4_scr):
    first_chunk = pl.program_id(0) == 0
    chunk = q_ref.shape[0]
    plane = chunk // DIL_MID
    srcs = (q_ref, kc_ref, kp_ref, vc_ref, vp_ref)
    for a, ref in enumerate(srcs):
        for p in range(DIL_MID):
            d4_scr[a, p] = ref[pl.ds(p, plane, stride=DIL_MID), :]

    def plane_rows(r, ub, dil):
        step = dil // DIL_MID
        lo = r // DIL_MID + step * BLOCK * ub
        return r % DIL_MID, (slice(lo, lo + BLOCK) if step == 1 else pl.ds(lo, BLOCK, stride=step))

    def tile(a, r, ub, dil):
        if dil == 1:
            return srcs[a][ub * BLOCK:(ub + 1) * BLOCK, :]
        p, rows = plane_rows(r, ub, dil)
        return d4_scr[a, p, rows, :]

    col = lax.broadcasted_iota(jnp.int32, (BLOCK, 2 * BLOCK), 1)
    ones = jnp.ones((2 * BLOCK, LANES), BF16)
    for bi, (_, dil) in enumerate(A_BRANCHES):
        nsub = chunk // (BLOCK * dil)
        bias = bias_ref[bi]
        for r in range(dil):
            k_prev = tile(2, r, nsub - 1, dil).astype(BF16)
            v_prev = tile(4, r, nsub - 1, dil).astype(BF16)
            for ub in range(nsub):
                q = (tile(0, r, ub, dil) * QK_SCALE_LOG2).astype(BF16)
                k_cur = tile(1, r, ub, dil).astype(BF16)
                v_cur = tile(3, r, ub, dil).astype(BF16)
                z = _dot_nt(q, jnp.concatenate([k_prev, k_cur], axis=0)) + bias
                if ub == 0:
                    z = jnp.where(jnp.logical_and(first_chunk, col < BLOCK), NEG, z)
                m = jnp.max(z, axis=-1, keepdims=True)
                e = jnp.exp2(z - m).astype(BF16)
                v_ext = jnp.concatenate([jnp.concatenate([v_prev, v_cur], axis=0), ones], axis=1)
                num = jnp.dot(e, v_ext, preferred_element_type=F32)
                if dil == 1:
                    rows = slice(ub * BLOCK, (ub + 1) * BLOCK)
                else:
                    p, rows = plane_rows(r, ub, dil)
                    rows = (slice(p * plane + rows.start, p * plane + rows.stop) if isinstance(rows, slice)
                            else pl.ds(p * plane + rows.start, BLOCK, stride=rows.stride))
                num_scr[bi, rows, :] = num[:, :HEAD_DIM]
                m_scr[bi, rows, :] = jnp.broadcast_to(m, (BLOCK, LANES))
                s_scr[bi, rows, :] = num[:, HEAD_DIM:]
                k_prev, v_prev = k_cur, v_cur
    quarter = plane // DIL_MID
    for p in range(DIL_MID):
        def stat(scr, bi):
            if bi == 0:
                return scr[0, pl.ds(p, plane, stride=DIL_MID), :]
            return scr[bi, p * plane:(p + 1) * plane, :]
        ms = [stat(m_scr, bi) for bi in range(len(A_BRANCHES))]
        m_all = functools.reduce(jnp.maximum, ms)
        top = jnp.zeros_like(m_all)
        bot = jnp.zeros_like(m_all)
        for bi in range(len(A_BRANCHES)):
            w = jnp.exp2(ms[bi] - m_all)
            top += w * stat(num_scr, bi)
            bot += w * stat(s_scr, bi)
        out = top / bot
        for t4 in range(DIL_MID):
            d4_scr[0, t4, pl.ds(p, quarter, stride=DIL_MID), :] = out[t4 * quarter:(t4 + 1) * quarter]
    for t4 in range(DIL_MID):
        o_ref[t4 * plane:(t4 + 1) * plane, :] = d4_scr[0, t4].astype(o_ref.dtype)


def _attn_a_prompt(qkv, bias_a):
    t = qkv.shape[0]
    n_chunks = t // CHUNK_A
    blk = (CHUNK_A, HEAD_DIM)
    prev = lambda b: jnp.maximum(b - 1, 0)
    blocks = (2 * (5 * _nbytes(blk, F32) + _nbytes((3, BLOCK, 2 * BLOCK), F32) + _nbytes(blk, BF16))
              + 14 * _nbytes(blk, F32))
    return pl.pallas_call(
        _attn_a_prompt_kernel,
        grid=(n_chunks, N_HEADS_A),
        in_specs=[pl.BlockSpec(blk, lambda b, h: (b, COL_QA + h)),
                  pl.BlockSpec(blk, lambda b, h: (b, COL_KA + h)),
                  pl.BlockSpec(blk, lambda b, h: (prev(b), COL_KA + h)),
                  pl.BlockSpec(blk, lambda b, h: (b, COL_VA + h)),
                  pl.BlockSpec(blk, lambda b, h: (prev(b), COL_VA + h)),
                  pl.BlockSpec((3, None, BLOCK, 2 * BLOCK), lambda b, h: (0, h, 0, 0))],
        out_specs=pl.BlockSpec(blk, lambda b, h: (b, h)),
        out_shape=jax.ShapeDtypeStruct((t, QA_W), BF16),
        scratch_shapes=[pltpu.VMEM((3, CHUNK_A, HEAD_DIM), F32)] * 3
        + [pltpu.VMEM((5, DIL_MID, CHUNK_A // DIL_MID, HEAD_DIM), F32)],
        compiler_params=pltpu.CompilerParams(dimension_semantics=("parallel", "parallel"),
                                             vmem_limit_bytes=_vmem_limit(blocks)),
        name="mixer_a_prompt",
    )(qkv, qkv, qkv, qkv, qkv, bias_a)


def _attn_b_prompt_kernel(sink_ref, q_ref, kc_ref, kp_ref, vc_ref, vp_ref, bias_ref, o_ref):
    first_chunk = pl.program_id(0) == 0
    g = pl.program_id(1)
    col = lax.broadcasted_iota(jnp.int32, (BLOCK, 2 * BLOCK), 1)
    k_prev = kp_ref[...].astype(BF16)
    v_prev = vp_ref[...].astype(BF16)
    for blk in range(q_ref.shape[0] // BLOCK):
        rows = slice(blk * BLOCK, (blk + 1) * BLOCK)
        k_cur = kc_ref[rows, :].astype(BF16)
        v_cur = vc_ref[rows, :].astype(BF16)
        kcat = jnp.concatenate([k_prev, k_cur], axis=0)
        vcat = jnp.concatenate([v_prev, v_cur], axis=0)
        for j in range(GROUP_B):
            cols = slice(j * HEAD_DIM, (j + 1) * HEAD_DIM)
            sink = sink_ref[g * GROUP_B + j]
            z = _dot_nt((q_ref[rows, cols] * QK_SCALE_LOG2).astype(BF16), kcat) + bias_ref[j]
            if blk == 0:
                z = jnp.where(jnp.logical_and(first_chunk, col < BLOCK), NEG, z)
            m = jnp.maximum(jnp.max(z, axis=-1, keepdims=True), sink)
            e = jnp.exp2(z - m)
            denom = jnp.sum(e, axis=-1, keepdims=True) + jnp.exp2(sink - m)
            num = jnp.dot(e.astype(BF16), vcat, preferred_element_type=F32)
            o_ref[rows, cols] = (num / denom).astype(o_ref.dtype)
        k_prev, v_prev = k_cur, v_cur


def _attn_b_prompt(qkv, bias_b, sinks):
    t = qkv.shape[0]
    n_chunks = t // CHUNK_B
    per = CHUNK_B // BLOCK
    qblk = (CHUNK_B, GROUP_B * HEAD_DIM)
    kblk = (CHUNK_B, HEAD_DIM)
    pblk = (BLOCK, HEAD_DIM)
    qcol = COL_QB // GROUP_B
    prev = lambda b: jnp.maximum(b * per - 1, 0)
    blocks = 2 * (_nbytes(qblk, F32) + 2 * _nbytes(kblk, F32) + 2 * _nbytes(pblk, F32)
                  + _nbytes((GROUP_B, BLOCK, 2 * BLOCK), F32) + _nbytes(qblk, BF16))
    return pl.pallas_call(
        _attn_b_prompt_kernel,
        grid=(n_chunks, N_KV_B),
        in_specs=[pl.BlockSpec(memory_space=pltpu.SMEM),
                  pl.BlockSpec(qblk, lambda b, g: (b, qcol + g)),
                  pl.BlockSpec(kblk, lambda b, g: (b, COL_KB + g)),
                  pl.BlockSpec(pblk, lambda b, g: (prev(b), COL_KB + g)),
                  pl.BlockSpec(kblk, lambda b, g: (b, COL_VB + g)),
                  pl.BlockSpec(pblk, lambda b, g: (prev(b), COL_VB + g)),
                  pl.BlockSpec((None, GROUP_B, BLOCK, 2 * BLOCK), lambda b, g: (g, 0, 0, 0))],
        out_specs=pl.BlockSpec(qblk, lambda b, g: (b, g)),
        out_shape=jax.ShapeDtypeStruct((t, QB_W), BF16),
        compiler_params=pltpu.CompilerParams(dimension_semantics=("parallel", "parallel"),
                                             vmem_limit_bytes=_vmem_limit(blocks)),
        name="mixer_b_prompt",
    )(sinks.astype(F32), qkv, qkv, qkv, qkv, qkv, bias_b)


def _heads_to_rows(x, n_heads):
    return jnp.concatenate([x[:, h * HEAD_DIM:(h + 1) * HEAD_DIM] for h in range(n_heads)], axis=0)


def _rows_to_heads(x, n_heads):
    s = x.shape[0] // n_heads
    return jnp.concatenate([x[h * s:(h + 1) * s, :] for h in range(n_heads)], axis=1)


def _attn_sample_kernel(qkv_ref, kx_ref, ky_ref, vx_ref, vy_ref, kb_ref, vb_ref,
                        mba_ref, mbb_ref, sink_ref, o_ref,
                        ka_scr, va_scr, kb_scr, vb_scr, z_scr):
    s_new = qkv_ref.shape[0] // SA_ROWS_PER_STEP
    for row in range(SA_ROWS_PER_STEP):
        tok = slice(row * s_new, (row + 1) * s_new)

        def new_rows(col0, n_heads, n_rows):
            new = _heads_to_rows(qkv_ref[tok, col0 * LANES:(col0 + n_heads) * LANES], n_heads)
            return jnp.concatenate([new, jnp.zeros((n_rows - new.shape[0], HEAD_DIM), F32)], axis=0).astype(BF16)

        def fill_a(scr, x_ref, y_ref, col0):
            scr[0:SA_X_COLS, :] = x_ref[row].reshape(SA_X_COLS, HEAD_DIM).astype(BF16)
            scr[SA_X_COLS:SA_X_COLS + SA_Y_COLS, :] = y_ref[row].reshape(SA_Y_COLS, HEAD_DIM).astype(BF16)
            scr[SA_X_COLS + SA_Y_COLS:SA_COLS, :] = new_rows(col0, N_HEADS_A, SA_NEW_COLS)

        def fill_b(scr, c_ref, col0):
            scr[0:SB_CACHE_COLS, :] = c_ref[row].astype(BF16)
            scr[SB_CACHE_COLS:SB_COLS, :] = new_rows(col0, N_KV_B, SB_NEW_COLS)

        def queries(col0, n_heads):
            q = _heads_to_rows(qkv_ref[tok, col0 * LANES:(col0 + n_heads) * LANES], n_heads)
            return (q * QK_SCALE_LOG2).astype(BF16)

        fill_a(ka_scr, kx_ref, ky_ref, COL_KA)
        fill_a(va_scr, vx_ref, vy_ref, COL_VA)
        fill_b(kb_scr, kb_ref, COL_KB)
        fill_b(vb_scr, vb_ref, COL_VB)

        qa = queries(COL_QA, N_HEADS_A)
        m = None
        for c0 in range(0, SA_COLS, SA_CHUNK):
            z = _dot_nt(qa, ka_scr[c0:c0 + SA_CHUNK, :]) + mba_ref[:, c0:c0 + SA_CHUNK]
            z_scr[:, c0:c0 + SA_CHUNK] = z
            zmax = jnp.max(z, axis=-1, keepdims=True)
            m = zmax if m is None else jnp.maximum(m, zmax)
        denom, oa = 0.0, 0.0
        for c0 in range(0, SA_COLS, SA_CHUNK):
            p = jnp.exp2(z_scr[:, c0:c0 + SA_CHUNK] - m)
            denom = denom + jnp.sum(p, axis=-1, keepdims=True)
            oa = oa + jnp.dot(p.astype(BF16), va_scr[c0:c0 + SA_CHUNK, :], preferred_element_type=F32)
        o_ref[tok, 0:QA_W] = _rows_to_heads(oa / denom, N_HEADS_A)

        zb = _dot_nt(queries(COL_QB, N_HEADS_B), kb_scr[...]) + mbb_ref[...]
        sink = sink_ref[:, 0:1]
        mb = jnp.maximum(jnp.max(zb, axis=-1, keepdims=True), sink)
        eb = jnp.exp2(zb - mb)
        denom_b = jnp.sum(eb, axis=-1, keepdims=True) + jnp.exp2(sink - mb)
        ob = jnp.dot(eb.astype(BF16), vb_scr[...], preferred_element_type=F32) / denom_b
        o_ref[tok, QA_W:QA_W + QB_W] = _rows_to_heads(ob, N_HEADS_B)


def _attn_sample(qkv, layer, cache_ak, cache_av, cache_bk, cache_bv, mb_a, mb_b, sink_rows, s_new):
    depth, db, la = cache_ak.shape[0], cache_ak.shape[1], cache_ak.shape[2]
    lb = cache_bk.shape[2]
    rps = SA_ROWS_PER_STEP
    ax_k = cache_ak.reshape(depth, db, la // DIL_MAX, DIL_MAX, N_HEADS_A, HEAD_DIM)
    ax_v = cache_av.reshape(depth, db, la // DIL_MAX, DIL_MAX, N_HEADS_A, HEAD_DIM)
    bk = cache_bk.reshape(depth, db, lb * N_KV_B, HEAD_DIM)
    bv = cache_bv.reshape(depth, db, lb * N_KV_B, HEAD_DIM)
    xspec = pl.BlockSpec((None, rps, SA_X_M, SUBLANES, N_HEADS_A, HEAD_DIM), lambda b: (layer, b, 0, 0, 0, 0))
    yspec = pl.BlockSpec((None, rps, SA_Y_POS, N_HEADS_A, HEAD_DIM),
                         lambda b: (layer, b, (la - SA_Y_POS) // SA_Y_POS, 0, 0))
    bspec = pl.BlockSpec((None, rps, SB_CACHE_COLS, HEAD_DIM), lambda b: (layer, b, 0, 0))
    rows_q = N_HEADS_A * s_new
    const = lambda b: (0, 0)
    blocks = (2 * rps * (_nbytes((s_new, IN_W), F32) + 2 * _nbytes((SA_X_COLS + SA_Y_COLS, HEAD_DIM), F32)
                         + 2 * _nbytes((SB_CACHE_COLS, HEAD_DIM), F32) + _nbytes((s_new, QA_W + QB_W), F32))
              + 2 * (_nbytes((rows_q, SA_COLS), F32) + _nbytes((rows_q, SB_COLS), F32) + _nbytes((rows_q, LANES), F32))
              + 2 * _nbytes((SA_COLS, HEAD_DIM), BF16) + 2 * _nbytes((SB_COLS, HEAD_DIM), BF16)
              + _nbytes((rows_q, SA_COLS), F32))
    return pl.pallas_call(
        _attn_sample_kernel,
        grid=(db // rps,),
        in_specs=[pl.BlockSpec((rps * s_new, IN_W), lambda b: (b, 0)),
                  xspec, yspec, xspec, yspec, bspec, bspec,
                  pl.BlockSpec((rows_q, SA_COLS), const),
                  pl.BlockSpec((rows_q, SB_COLS), const),
                  pl.BlockSpec((rows_q, LANES), const)],
        out_specs=pl.BlockSpec((rps * s_new, QA_W + QB_W), lambda b: (b, 0)),
        out_shape=jax.ShapeDtypeStruct((db * s_new, QA_W + QB_W), F32),
        scratch_shapes=[pltpu.VMEM((SA_COLS, HEAD_DIM), BF16), pltpu.VMEM((SA_COLS, HEAD_DIM), BF16),
                        pltpu.VMEM((SB_COLS, HEAD_DIM), BF16), pltpu.VMEM((SB_COLS, HEAD_DIM), BF16),
                        pltpu.VMEM((rows_q, SA_COLS), F32)],
        compiler_params=pltpu.CompilerParams(dimension_semantics=("parallel",),
                                             vmem_limit_bytes=_vmem_limit(blocks)),
        name="mixers_sample",
    )(qkv, ax_k, cache_ak, ax_v, cache_av, bk, bv, mb_a, mb_b, sink_rows)


def _outproj_kernel(x_ref, ma_ref, mb_ref, wa_ref, wb_ref, gt_ref, o_ref):
    acc = jnp.dot(ma_ref[...].astype(BF16), wa_ref[...], preferred_element_type=F32)
    acc += jnp.dot(mb_ref[...].astype(BF16), wb_ref[...], preferred_element_type=F32)
    o_ref[...] = x_ref[...] + gt_ref[...] * acc.reshape(o_ref.shape)


def _out_projection(x3, mix_a, mix_b, col_a, col_b, layer, w_out_bf, mod3, bb, rb):
    nb, r, d = x3.shape
    tm = bb * rb
    tn = d
    rblocks = r // rb
    n_m = (nb // bb) * rblocks
    kh = QA_W
    blocks = 2 * (2 * _nbytes((tm, tn), F32) + _nbytes((tm, kh), mix_a.dtype) + _nbytes((tm, kh), mix_b.dtype)
                  + 2 * _nbytes((kh, tn), BF16) + _nbytes((bb, SUBLANES, tn), F32))
    return pl.pallas_call(
        _outproj_kernel,
        grid=(n_m, d // tn),
        in_specs=[pl.BlockSpec((bb, rb, tn), lambda i, j: (i // rblocks, i % rblocks, j)),
                  pl.BlockSpec((tm, kh), lambda i, j: (i, col_a)),
                  pl.BlockSpec((tm, kh), lambda i, j: (i, col_b)),
                  pl.BlockSpec((None, kh, tn), lambda i, j: (layer, 0, j)),
                  pl.BlockSpec((None, kh, tn), lambda i, j: (layer, 1, j)),
                  pl.BlockSpec((bb, 1, tn), lambda i, j: (i // rblocks, 0, 2 * (d // tn) + j))],
        out_specs=pl.BlockSpec((bb, rb, tn), lambda i, j: (i // rblocks, i % rblocks, j)),
        out_shape=jax.ShapeDtypeStruct(x3.shape, F32),
        compiler_params=pltpu.CompilerParams(dimension_semantics=("parallel", "parallel"),
                                             vmem_limit_bytes=_vmem_limit(blocks)),
        name="out_projection",
    )(x3, mix_a, mix_b, w_out_bf, w_out_bf, mod3)


def _ffn_kernel(x_ref, sc_ref, sh_ref, gt_ref, g_ref, wg_ref, wu_ref, wd_ref, wc_ref, bc_ref, hist_ref,
                o_ref, cs_ref, h_scr, gate_scr, up_scr, act_scr, carry_scr, *, rblocks, chunk_seqs, chunk_rows):
    i = pl.program_id(0)
    f = pl.program_id(1)
    n_f = pl.num_programs(1)
    bb, rb, d = x_ref.shape
    tm, tf = bb * rb, wg_ref.shape[1]

    @pl.when(f == 0)
    def _():
        _norm_modulate_rows(x_ref, g_ref, sc_ref, sh_ref, h_scr)
        o_ref[...] = jnp.zeros(o_ref.shape, F32)

    if rblocks > 1:
        @pl.when(i % rblocks == 0)
        def _():
            carry_scr[f] = hist_ref[...]
        history = lambda s0, n: carry_scr[f]
    else:
        history = lambda s0, n: hist_ref[s0:s0 + n]

    h = h_scr[...]
    gate_scr[0:SUBLANES, :] = jnp.zeros((SUBLANES, tf), F32)
    gate_scr[SUBLANES:, :] = jnp.dot(h, wg_ref[...], preferred_element_type=F32)
    up_scr[...] = jnp.dot(h, wu_ref[...], preferred_element_type=F32)

    n_rows = chunk_seqs * chunk_rows
    shape = (chunk_seqs, chunk_rows, tf)
    t = lax.broadcasted_iota(jnp.int32, shape, 1)
    for r0 in range(0, tm, n_rows):
        def shifted(s):
            lo = SUBLANES + r0 - s
            return gate_scr[lo:lo + n_rows, :].reshape(shape)
        g0, g1, g2 = shifted(0), shifted(1), shifted(2)
        if r0 % rb == 0:
            hist = history(r0 // rb, chunk_seqs)
            g1 = jnp.where(t >= 1, g1, hist[:, 1:2, :])
            g2 = jnp.where(t >= 2, g2, jnp.where(t == 0, hist[:, 0:1, :], hist[:, 1:2, :]))
        gc = bc_ref[...] + wc_ref[0:1, :] * g2 + wc_ref[1:2, :] * g1 + wc_ref[2:3, :] * g0
        act = jax.nn.gelu(gc, approximate=True) * up_scr[r0:r0 + n_rows, :].reshape(shape)
        act_scr[r0:r0 + n_rows, :] = act.reshape(n_rows, tf).astype(BF16)
        if (r0 + n_rows) % rb == 0:
            s0 = (r0 + n_rows) // rb - chunk_seqs
            tail = g0[:, chunk_rows - (CONV_W - 1):, :]
            cs_ref[s0:s0 + chunk_seqs] = tail
            if rblocks > 1:
                carry_scr[f] = tail
    o_ref[...] += jnp.dot(act_scr[...], wd_ref[...], preferred_element_type=F32).reshape(bb, rb, d)

    @pl.when(f == n_f - 1)
    def _():
        o_ref[...] = x_ref[...] + gt_ref[...] * o_ref[...]


FFN_CHUNK_ROWS = 64


def _conv_ffn(x3, mod3, layer, g_ffn, wg_bf, wu_bf, wd_bf, w_conv, b_conv, hist, bb, rb):
    nb, r, d = x3.shape
    f_pad = wg_bf.shape[2]
    tf = FF_TILE
    n_f = f_pad // tf
    tm = bb * rb
    rblocks = r // rb
    assert rblocks == 1 or bb == 1
    n_m = (nb // bb) * rblocks
    carry_slots = n_f if rblocks > 1 else 1
    chunk_rows = min(rb, FFN_CHUNK_ROWS)
    chunk_seqs = FFN_CHUNK_ROWS // chunk_rows
    assert rb % chunk_rows == 0 and bb % chunk_seqs == 0 and chunk_rows >= CONV_W - 1
    modmap = lambda k: (lambda i, f: (i // rblocks, 0, k))
    blocks = (2 * (2 * _nbytes((tm, d), F32) + 2 * _nbytes((d, tf), BF16) + _nbytes((tf, d), BF16)
                   + 2 * _nbytes((bb, SUBLANES, tf), F32) + 3 * _nbytes((bb, SUBLANES, d), F32))
              + _nbytes((tm, d), BF16) + carry_slots * _nbytes((bb, SUBLANES, tf), F32)
              + 3 * _nbytes((tm + SUBLANES, tf), F32))
    y, tails = pl.pallas_call(
        functools.partial(_ffn_kernel, rblocks=rblocks, chunk_seqs=chunk_seqs, chunk_rows=chunk_rows),
        grid=(n_m, n_f),
        in_specs=[pl.BlockSpec((bb, rb, d), lambda i, f: (i // rblocks, i % rblocks, 0)),
                  pl.BlockSpec((bb, 1, d), modmap(4)),
                  pl.BlockSpec((bb, 1, d), modmap(3)),
                  pl.BlockSpec((bb, 1, d), modmap(5)),
                  pl.BlockSpec((1, d), lambda i, f: (0, 0)),
                  pl.BlockSpec((None, d, tf), lambda i, f: (layer, 0, f)),
                  pl.BlockSpec((None, d, tf), lambda i, f: (layer, 0, f)),
                  pl.BlockSpec((None, tf, d), lambda i, f: (layer, f, 0)),
                  pl.BlockSpec((None, CONV_W, tf), lambda i, f: (layer, 0, f)),
                  pl.BlockSpec((None, 1, tf), lambda i, f: (layer, 0, f)),
                  pl.BlockSpec((bb, CONV_W - 1, tf), lambda i, f: (i // rblocks, 0, f))],
        out_specs=[pl.BlockSpec((bb, rb, d), lambda i, f: (i // rblocks, i % rblocks, 0)),
                   pl.BlockSpec((bb, CONV_W - 1, tf), lambda i, f: (i, 0, f))],
        out_shape=[jax.ShapeDtypeStruct(x3.shape, F32),
                   jax.ShapeDtypeStruct((n_m * bb, CONV_W - 1, f_pad), F32)],
        scratch_shapes=[pltpu.VMEM((tm, d), BF16),
                        pltpu.VMEM((tm + SUBLANES, tf), F32),
                        pltpu.VMEM((tm, tf), F32),
                        pltpu.VMEM((tm, tf), BF16),
                        pltpu.VMEM((carry_slots, bb, CONV_W - 1, tf), F32)],
        compiler_params=pltpu.CompilerParams(dimension_semantics=("arbitrary", "arbitrary"),
                                             vmem_limit_bytes=_vmem_limit(blocks)),
        name="conv_ffn",
    )(x3, mod3, mod3, mod3, g_ffn.reshape(1, d), wg_bf, wu_bf, wd_bf, w_conv, b_conv, hist)
    state = tails.reshape(nb // bb, rblocks, bb, CONV_W - 1, f_pad)[:, rblocks - 1]
    return y, state.reshape(nb, CONV_W - 1, f_pad)


def _pad_last(a, n):
    return jnp.pad(a, [(0, 0)] * (a.ndim - 1) + [(0, n - a.shape[-1])])


def _cast_pad_cols_kernel(x_ref, o_ref):
    n = x_ref.shape[1]
    o_ref[:, :n] = x_ref[...].astype(BF16)
    o_ref[:, n:] = jnp.zeros((o_ref.shape[0], o_ref.shape[1] - n), BF16)


def _cast_pad_cols(w, n_pad):
    depth, rows, n = w.shape
    tr = 256
    blocks = 2 * (_nbytes((tr, n), F32) + _nbytes((tr, n_pad), BF16))
    return pl.pallas_call(
        _cast_pad_cols_kernel,
        grid=(depth, rows // tr),
        in_specs=[pl.BlockSpec((None, tr, n), lambda l, i: (l, i, 0))],
        out_specs=pl.BlockSpec((None, tr, n_pad), lambda l, i: (l, i, 0)),
        out_shape=jax.ShapeDtypeStruct((depth, rows, n_pad), BF16),
        compiler_params=pltpu.CompilerParams(dimension_semantics=("parallel", "parallel"),
                                             vmem_limit_bytes=_vmem_limit(blocks)),
        name="cast_pad_cols",
    )(w)


def kernel(x_prompt, x_sample, c_prompt, c_sample, cache_a_k, cache_a_v, cache_b_k, cache_b_v, state_conv, rel_bias, w_ada, b_ada, g_attn, g_ffn, w_in, g_qn_a, g_kn_a, g_qn_b, g_kn_b, sinks, w_out, w_gate, w_up, w_conv, b_conv, w_down):
    depth = w_in.shape[0]
    batch, seq, d = x_prompt.shape
    db, s_new, _ = x_sample.shape
    d_ff = w_gate.shape[2]
    f_pad = -(-d_ff // FF_TILE) * FF_TILE
    la, lb = cache_a_k.shape[2], cache_b_k.shape[2]
    assert batch == 1 and seq % CHUNK_A == 0 and s_new == SUBLANES

    pb_rev = _bias_by_distance_rev(rel_bias, max(WIN_A + 1, la + s_new)) * LOG2E
    bias_a, bias_b = _prompt_bias_tables(pb_rev)
    mb_a, mb_b = _sample_tables(pb_rev, s_new, la, lb)
    sinks2 = sinks.astype(F32) * LOG2E

    c_all = jnp.concatenate([c_prompt, jnp.zeros((SUBLANES - batch, d), F32), c_sample], axis=0)

    ones = jnp.ones((HEAD_DIM,), F32)
    flag = jnp.concatenate([jnp.ones((2 * QA_W,), F32), jnp.zeros((QA_W,), F32), jnp.ones((QB_W + KB_W,), F32),
                            jnp.zeros((KB_W,), F32)]).reshape(1, IN_W)

    w_in_bf = w_in.astype(BF16)
    w_out_bf = w_out.astype(BF16)
    wg_bf = _cast_pad_cols(w_gate, f_pad)
    wu_bf = _cast_pad_cols(w_up, f_pad)
    wd_bf = jnp.pad(w_down.astype(BF16), ((0, 0), (0, f_pad - d_ff), (0, 0)))
    wc = _pad_last(w_conv, f_pad)
    bc = _pad_last(b_conv, f_pad).reshape(depth, 1, f_pad)
    mod_all = _modulation(c_all, w_ada, b_ada)

    xp = x_prompt
    xs = x_sample
    hist_p = jnp.zeros((batch, CONV_W - 1, f_pad), F32)
    outs = [[] for _ in range(10)]
    for l in range(depth):
        gain = jnp.concatenate([jnp.tile(g_qn_a[l], N_HEADS_A), jnp.tile(g_kn_a[l], N_HEADS_A),
                                jnp.tile(ones, N_HEADS_A), jnp.tile(g_qn_b[l], N_HEADS_B),
                                jnp.tile(g_kn_b[l], N_KV_B), jnp.tile(ones, N_KV_B)]).reshape(1, IN_W)
        sink_rows = jnp.broadcast_to(jnp.repeat(sinks2[l], s_new)[:, None], (N_HEADS_B * s_new, LANES))
        mod_p = mod_all[l, 0:batch].reshape(batch, 1, 6 * d)
        mod_s = mod_all[l, SUBLANES:].reshape(db, 1, 6 * d)

        qkv_p = _projection(xp, mod_p, l, g_attn[l], w_in_bf, gain, flag, 1, 1024)
        mix_pa = _attn_a_prompt(qkv_p, bias_a)
        mix_pb = _attn_b_prompt(qkv_p, bias_b, sinks2[l])
        xp = _out_projection(xp, mix_pa, mix_pb, 0, 0, l, w_out_bf, mod_p, 1, 512)
        xp, conv_p = _conv_ffn(xp, mod_p, l, g_ffn[l], wg_bf, wu_bf, wd_bf, wc, bc, hist_p, 1, 1024)
        keep_a, keep_b = min(WIN_A, seq), min(WIN_B, seq)
        outs[0].append(qkv_p[seq - keep_a:, COL_KA * LANES:COL_KA * LANES + QA_W].reshape(batch, keep_a, N_HEADS_A, HEAD_DIM))
        outs[1].append(qkv_p[seq - keep_a:, COL_VA * LANES:COL_VA * LANES + QA_W].reshape(batch, keep_a, N_HEADS_A, HEAD_DIM))
        outs[2].append(qkv_p[seq - keep_b:, COL_KB * LANES:COL_KB * LANES + KB_W].reshape(batch, keep_b, N_KV_B, HEAD_DIM))
        outs[3].append(qkv_p[seq - keep_b:, COL_VB * LANES:COL_VB * LANES + KB_W].reshape(batch, keep_b, N_KV_B, HEAD_DIM))
        outs[4].append(conv_p[:, :, :d_ff])

        qkv_s = _projection(xs, mod_s, l, g_attn[l], w_in_bf, gain, flag, db, s_new)
        mix_s = _attn_sample(qkv_s, l, cache_a_k, cache_a_v, cache_b_k, cache_b_v,
                             mb_a, mb_b, sink_rows, s_new)
        xs = _out_projection(xs, mix_s, mix_s, 0, 1, l, w_out_bf, mod_s, db // 2, s_new)
        xs, conv_s = _conv_ffn(xs, mod_s, l, g_ffn[l], wg_bf, wu_bf, wd_bf, wc, bc,
                               _pad_last(state_conv[l], f_pad), db // 2, s_new)
        outs[5].append(qkv_s[:, COL_KA * LANES:COL_KA * LANES + QA_W].reshape(db, s_new, N_HEADS_A, HEAD_DIM))
        outs[6].append(qkv_s[:, COL_VA * LANES:COL_VA * LANES + QA_W].reshape(db, s_new, N_HEADS_A, HEAD_DIM))
        outs[7].append(qkv_s[:, COL_KB * LANES:COL_KB * LANES + KB_W].reshape(db, s_new, N_KV_B, HEAD_DIM))
        outs[8].append(qkv_s[:, COL_VB * LANES:COL_VB * LANES + KB_W].reshape(db, s_new, N_KV_B, HEAD_DIM))
        outs[9].append(conv_s[:, :, :d_ff])

    return (xp, xs) + tuple(jnp.stack(o) for o in outs)
```
